```python
import math
import jax, jax.numpy as jnp
from jax import lax
import numpy as np

D_MODEL = 1024
BATCH = 4
SEQ = 4096
DEPTH = 4
DEC_BATCH = 8
DEC_SEQ = 4096
PAST_LEN = 128

GRID_W = 64
N_MIXERS = 2
N_SSM_LAYERS = (DEPTH + 1) // 2
N_ATTN_LAYERS = DEPTH // 2
D_FF = 2816
SSM_EXPAND = 2
D_INNER = SSM_EXPAND * D_MODEL
SSM_HEAD_DIM = 64
SSM_HEADS = D_INNER // SSM_HEAD_DIM
SSM_GROUPS = 8
SSM_HEADS_PER_GROUP = SSM_HEADS // SSM_GROUPS
D_STATE = 128
D_CONV = 5
CONV_DIM = D_INNER + 2 * SSM_GROUPS * D_STATE
SSM_IN_DIM = D_INNER + CONV_DIM + 2 * SSM_HEADS
SSD_CHUNK = 128
ATT_HEAD_DIM = 64
N_HEADS = D_MODEL // ATT_HEAD_DIM
N_KV_HEADS = 4
KV_REP = N_HEADS // N_KV_HEADS
QKV_DIM = (N_HEADS + 2 * N_KV_HEADS) * ATT_HEAD_DIM
AXIS_DIM = ATT_HEAD_DIM // 2
ROPE_THETA = 10000.0
Q_BLOCK = 128
EPS = 1e-6

kernel_name = 'hybrid_ssd_axial_gqa_encoder'


def rms_norm(x, g):
    xf = x.astype(jnp.float32)
    out = xf * lax.rsqrt(jnp.mean(xf * xf, axis=-1, keepdims=True) + EPS)
    return (out * g.astype(jnp.float32)).astype(x.dtype)


def swiglu(h, w_gate, w_up, w_down):
    return (jax.nn.silu(h @ w_gate) * (h @ w_up)) @ w_down


def depthwise_centred_conv(x, w, b):
    y = lax.conv_general_dilated(
        x, w[:, None, :].astype(x.dtype), window_strides=(1,),
        padding=[(D_CONV // 2, D_CONV // 2)],
        dimension_numbers=('NWC', 'WIO', 'NWC'),
        feature_group_count=x.shape[-1])
    return y + b.astype(x.dtype)


def ssd_scan(x, dt, A, Bm, Cm):
    b, T = x.shape[0], x.shape[1]
    nc = T // SSD_CHUNK

    def chunk(a):
        a = a.astype(jnp.float32)
        return jnp.moveaxis(a.reshape((b, nc, SSD_CHUNK) + a.shape[2:]), 1, 0)

    xc, dtc, Bc, Cc = chunk(x), chunk(dt), chunk(Bm), chunk(Cm)
    A = A.astype(jnp.float32)
    causal = jnp.tril(jnp.ones((SSD_CHUNK, SSD_CHUNK), dtype=bool))[None, :, :, None, None]

    def step(S, inp):
        xq, dtq, Bq, Cq = inp
        cs = jnp.cumsum(dtq * A, axis=1)
        diff = cs[:, :, None] - cs[:, None, :]
        L = jnp.exp(jnp.where(causal, diff, -jnp.inf))
        cb = jnp.einsum('bign,bjgn->bijg', Cq, Bq)
        y_in = jnp.einsum('bijg,bijgr,bjgr,bjgrp->bigrp', cb, L, dtq, xq)
        y_st = jnp.einsum('bign,bgrpn,bigr->bigrp', Cq, S, jnp.exp(cs))
        w_end = jnp.exp(cs[:, -1:] - cs) * dtq
        S_new = S * jnp.exp(cs[:, -1])[..., None, None] + jnp.einsum(
            'bjgn,bjgr,bjgrp->bgrpn', Bq, w_end, xq)
        return S_new, y_in + y_st

    S0 = jnp.zeros((b, SSM_GROUPS, SSM_HEADS_PER_GROUP, SSM_HEAD_DIM, D_STATE), jnp.float32)
    _, yc = lax.scan(step, S0, (xc, dtc, Bc, Cc))
    return jnp.moveaxis(yc, 0, 1).reshape(x.shape)


def bidirectional_ssd_mixer(h, w_in, conv_w, conv_b, dt_bias, A_log, D, norm_g, w_out):
    b, T, _ = h.shape
    proj = h @ w_in
    z = proj[..., :D_INNER]
    xbc = proj[..., D_INNER:D_INNER + CONV_DIM]
    dt_raw = proj[..., D_INNER + CONV_DIM:].reshape(b, T, 2, SSM_HEADS)
    xbc = jax.nn.silu(depthwise_centred_conv(xbc, conv_w, conv_b))
    xs = xbc[..., :D_INNER].reshape(b, T, SSM_GROUPS, SSM_HEADS_PER_GROUP, SSM_HEAD_DIM)
    Bm = xbc[..., D_INNER:D_INNER + SSM_GROUPS * D_STATE].reshape(b, T, SSM_GROUPS, D_STATE)
    Cm = xbc[..., D_INNER + SSM_GROUPS * D_STATE:].reshape(b, T, SSM_GROUPS, D_STATE)
    dt = jax.nn.softplus(dt_raw.astype(jnp.float32) + dt_bias.astype(jnp.float32))
    dt = dt.reshape(b, T, 2, SSM_GROUPS, SSM_HEADS_PER_GROUP)
    A = -jnp.exp(A_log.astype(jnp.float32)).reshape(2, SSM_GROUPS, SSM_HEADS_PER_GROUP)
    flip = lambda a: jnp.flip(a, axis=1)
    y_f = ssd_scan(xs, dt[:, :, 0], A[0], Bm, Cm)
    y_b = flip(ssd_scan(flip(xs), flip(dt[:, :, 1]), A[1], flip(Bm), flip(Cm)))
    Dg = D.astype(jnp.float32).reshape(SSM_GROUPS, SSM_HEADS_PER_GROUP)[:, :, None]
    y = y_f + y_b + Dg * xs.astype(jnp.float32)
    y = y.reshape(b, T, D_INNER) * jax.nn.silu(z.astype(jnp.float32))
    yg = y.reshape(b, T, SSM_GROUPS, D_INNER // SSM_GROUPS)
    yg = yg * lax.rsqrt(jnp.mean(yg * yg, axis=-1, keepdims=True) + EPS)
    y = yg.reshape(b, T, D_INNER) * norm_g.astype(jnp.float32)
    return y.astype(h.dtype) @ w_out


def axial_rope_tables(T):
    rows = T // GRID_W
    r_idx, c_idx = jnp.meshgrid(jnp.arange(rows), jnp.arange(GRID_W), indexing='ij')
    r_idx = r_idx.reshape(-1).astype(jnp.float32)
    c_idx = c_idx.reshape(-1).astype(jnp.float32)
    inv_freq = ROPE_THETA ** (-jnp.arange(0, AXIS_DIM, 2, dtype=jnp.float32) / AXIS_DIM)
    ang = jnp.stack([r_idx[:, None] * inv_freq, c_idx[:, None] * inv_freq], axis=1)
    return jnp.cos(ang)[:, :, None, :], jnp.sin(ang)[:, :, None, :]


def apply_axial_rope(x, cos, sin):
    xf = x.astype(jnp.float32)
    xr = xf.reshape(x.shape[:-1] + (2, 2, AXIS_DIM // 2))
    rot = jnp.stack([-xr[..., 1, :], xr[..., 0, :]], axis=-2)
    bshape = (1, x.shape[1]) + (1,) * (x.ndim - 3) + (2, 1, AXIS_DIM // 2)
    out = xr * cos.reshape(bshape) + rot * sin.reshape(bshape)
    return out.reshape(x.shape).astype(x.dtype)


def axial_gqa_attention(h, w_qkv, q_gain, k_gain, w_out):
    b, T, _ = h.shape
    qd, kd = N_HEADS * ATT_HEAD_DIM, N_KV_HEADS * ATT_HEAD_DIM
    qkv = h @ w_qkv
    q = qkv[..., :qd].reshape(b, T, N_KV_HEADS, KV_REP, ATT_HEAD_DIM)
    k = qkv[..., qd:qd + kd].reshape(b, T, N_KV_HEADS, ATT_HEAD_DIM)
    v = qkv[..., qd + kd:].reshape(b, T, N_KV_HEADS, ATT_HEAD_DIM)
    cos, sin = axial_rope_tables(T)
    q = apply_axial_rope(rms_norm(q, q_gain), cos, sin)
    k = apply_axial_rope(rms_norm(k, k_gain), cos, sin)
    n_blocks = T // Q_BLOCK
    q_blocks = jnp.moveaxis(q.reshape(b, n_blocks, Q_BLOCK, N_KV_HEADS, KV_REP, ATT_HEAD_DIM), 1, 0)
    scale = ATT_HEAD_DIM ** -0.5

    def attend(qb):
        s = jnp.einsum('bqgrd,bkgd->bgrqk', qb, k).astype(jnp.float32) * scale
        p = jax.nn.softmax(s, axis=-1).astype(v.dtype)
        return jnp.einsum('bgrqk,bkgd->bqgrd', p, v)

    o = lax.map(attend, q_blocks)
    o = jnp.moveaxis(o, 0, 1).reshape(b, T, qd)
    return o @ w_out


def trunk(x, norm_g, ffn_w_gate, ffn_w_up, ffn_w_down,
          ssm_w_in, ssm_conv_w, ssm_conv_b, ssm_dt_bias, ssm_A_log, ssm_D, ssm_norm_g, ssm_w_out,
          attn_w_qkv, attn_q_norm, attn_k_norm, attn_w_out, final_norm):
    for i in range(DEPTH):
        j = i // N_MIXERS
        x = x + 0.5 * swiglu(rms_norm(x, norm_g[i, 0]), ffn_w_gate[i, 0], ffn_w_up[i, 0], ffn_w_down[i, 0])
        h = rms_norm(x, norm_g[i, 1])
        if i % N_MIXERS == 0:
            x = x + bidirectional_ssd_mixer(h, ssm_w_in[j], ssm_conv_w[j], ssm_conv_b[j], ssm_dt_bias[j],
                                            ssm_A_log[j], ssm_D[j], ssm_norm_g[j], ssm_w_out[j])
        else:
            x = x + axial_gqa_attention(h, attn_w_qkv[j], attn_q_norm[j], attn_k_norm[j], attn_w_out[j])
        x = x + 0.5 * swiglu(rms_norm(x, norm_g[i, 2]), ffn_w_gate[i, 1], ffn_w_up[i, 1], ffn_w_down[i, 1])
    return rms_norm(x, final_norm)


def setup_inputs(seed: int = 0) -> dict:
    key = jax.random.key(seed)
    ks = jax.random.split(key, 20)
    nrm = lambda k, shape, s: jax.random.normal(k, shape, jnp.float32) * s
    x_prompt = nrm(ks[0], (BATCH, SEQ, D_MODEL), 1.0)
    x_sample = nrm(ks[1], (DEC_BATCH, DEC_SEQ, D_MODEL), 1.0)
    norm_g = 1.0 + nrm(ks[2], (DEPTH, 3, D_MODEL), 0.02)
    ffn_w_gate = nrm(ks[3], (DEPTH, 2, D_MODEL, D_FF), D_MODEL ** -0.5)
    ffn_w_up = nrm(ks[4], (DEPTH, 2, D_MODEL, D_FF), D_MODEL ** -0.5)
    ffn_w_down = nrm(ks[5], (DEPTH, 2, D_FF, D_MODEL), D_FF ** -0.5)
    ssm_w_in = nrm(ks[6], (N_SSM_LAYERS, D_MODEL, SSM_IN_DIM), D_MODEL ** -0.5)
    ssm_conv_w = nrm(ks[7], (N_SSM_LAYERS, D_CONV, CONV_DIM), D_CONV ** -0.5)
    ssm_conv_b = nrm(ks[8], (N_SSM_LAYERS, CONV_DIM), 0.02)
    dt0 = jnp.exp(jax.random.uniform(ks[9], (N_SSM_LAYERS, 2, SSM_HEADS), jnp.float32,
                                     minval=math.log(1e-3), maxval=math.log(1e-1)))
    ssm_dt_bias = dt0 + jnp.log(-jnp.expm1(-dt0))
    ssm_A_log = jnp.log(jax.random.uniform(ks[10], (N_SSM_LAYERS, 2, SSM_HEADS), jnp.float32,
                                           minval=1.0, maxval=16.0))
    ssm_D = 1.0 + nrm(ks[11], (N_SSM_LAYERS, SSM_HEADS), 0.1)
    ssm_norm_g = 1.0 + nrm(ks[12], (N_SSM_LAYERS, D_INNER), 0.02)
    ssm_w_out = nrm(ks[13], (N_SSM_LAYERS, D_INNER, D_MODEL), D_INNER ** -0.5)
    attn_w_qkv = nrm(ks[14], (N_ATTN_LAYERS, D_MODEL, QKV_DIM), D_MODEL ** -0.5)
    attn_q_norm = 1.0 + nrm(ks[15], (N_ATTN_LAYERS, ATT_HEAD_DIM), 0.02)
    attn_k_norm = 1.0 + nrm(ks[16], (N_ATTN_LAYERS, ATT_HEAD_DIM), 0.02)
    attn_w_out = nrm(ks[17], (N_ATTN_LAYERS, N_HEADS * ATT_HEAD_DIM, D_MODEL), (N_HEADS * ATT_HEAD_DIM) ** -0.5)
    final_norm = 1.0 + nrm(ks[18], (D_MODEL,), 0.02)
    return {'x_prompt': x_prompt, 'x_sample': x_sample, 'norm_g': norm_g,
            'ffn_w_gate': ffn_w_gate, 'ffn_w_up': ffn_w_up, 'ffn_w_down': ffn_w_down,
            'ssm_w_in': ssm_w_in, 'ssm_conv_w': ssm_conv_w, 'ssm_conv_b': ssm_conv_b,
            'ssm_dt_bias': ssm_dt_bias, 'ssm_A_log': ssm_A_log, 'ssm_D': ssm_D,
            'ssm_norm_g': ssm_norm_g, 'ssm_w_out': ssm_w_out,
            'attn_w_qkv': attn_w_qkv, 'attn_q_norm': attn_q_norm, 'attn_k_norm': attn_k_norm,
            'attn_w_out': attn_w_out, 'final_norm': final_norm}


def reference(x_prompt, x_sample, norm_g, ffn_w_gate, ffn_w_up, ffn_w_down,
              ssm_w_in, ssm_conv_w, ssm_conv_b, ssm_dt_bias, ssm_A_log, ssm_D, ssm_norm_g, ssm_w_out,
              attn_w_qkv, attn_q_norm, attn_k_norm, attn_w_out, final_norm):
    y_prompt = trunk(x_prompt, norm_g, ffn_w_gate, ffn_w_up, ffn_w_down,
                     ssm_w_in, ssm_conv_w, ssm_conv_b, ssm_dt_bias, ssm_A_log, ssm_D, ssm_norm_g, ssm_w_out,
                     attn_w_qkv, attn_q_norm, attn_k_norm, attn_w_out, final_norm)
    y_sample = trunk(x_sample, norm_g, ffn_w_gate, ffn_w_up, ffn_w_down,
                     ssm_w_in, ssm_conv_w, ssm_conv_b, ssm_dt_bias, ssm_A_log, ssm_D, ssm_norm_g, ssm_w_out,
                     attn_w_qkv, attn_q_norm, attn_k_norm, attn_w_out, final_norm)
    return (y_prompt, y_sample)
```

```python
import functools
import math

import jax
import jax.numpy as jnp
from jax import lax
from jax.experimental import pallas as pl
from jax.experimental.pallas import tpu as pltpu

F32 = jnp.float32
BF16 = jnp.bfloat16

D_MODEL = 1024
GRID_W = 64
D_FF = 2816
D_INNER = 2048
SSM_HEAD_DIM = 64
SSM_HEADS = 32
SSM_GROUPS = 8
HEADS_PER_GROUP = 4
GROUP_DIM = HEADS_PER_GROUP * SSM_HEAD_DIM
D_STATE = 128
D_CONV = 5
CONV_DIM = D_INNER + 2 * SSM_GROUPS * D_STATE
ATT_HEAD_DIM = 64
N_HEADS = 16
N_KV_HEADS = 4
KV_REP = 4
AXIS_DIM = 32
ROPE_THETA = 10000.0
EPS = 1e-6

LANES = 128
SUBLANES = 8
VMEM_LIMIT = 56 * 1024 * 1024

TOKEN_TILE = 512
SSD_CHUNK = 128
CONV_ROWS = 256
CONV_HALO = 8
ATT_Q_TILE = 256
ATT_K_TILE = 512
FF_SPLITS = (1024, 1024, 768)

_RESIDENT = dict(pipeline_mode=pl.Buffered(1))


def _params(n_axes):
    return pltpu.CompilerParams(dimension_semantics=("parallel",) * n_axes,
                                vmem_limit_bytes=VMEM_LIMIT)


def _sigmoid(x):
    return 1.0 / (1.0 + jnp.exp(-x))


def _rms(x, g):
    return x * lax.rsqrt(jnp.mean(x * x, axis=-1, keepdims=True) + EPS) * g


def _ffn_kernel(x_ref, g_ref, wg_ref, wu_ref, wd_ref, fg_ref, o_ref, *, final):
    x = x_ref[...]
    h = _rms(x, g_ref[...]).astype(BF16)
    acc = None
    c0 = 0
    for cw in FF_SPLITS:
        gate = jnp.dot(h, wg_ref[:, c0:c0 + cw], preferred_element_type=F32)
        up = jnp.dot(h, wu_ref[:, c0:c0 + cw], preferred_element_type=F32)
        a = (gate * _sigmoid(gate) * up).astype(BF16)
        part = jnp.dot(a, wd_ref[c0:c0 + cw, :], preferred_element_type=F32)
        acc = part if acc is None else acc + part
        c0 += cw
    y = x + 0.5 * acc
    if final:
        y = _rms(y, fg_ref[...])
    o_ref[...] = y


def _ffn(x, g, wg, wu, wd, fg, final):
    n = x.shape[0]
    tm = min(TOKEN_TILE, n)
    row = lambda i: (i, 0)
    fixed = lambda i: (0, 0)
    return pl.pallas_call(
        functools.partial(_ffn_kernel, final=final),
        grid=(n // tm,),
        in_specs=[pl.BlockSpec((tm, D_MODEL), row),
                  pl.BlockSpec((1, D_MODEL), fixed, **_RESIDENT),
                  pl.BlockSpec((D_MODEL, D_FF), fixed, **_RESIDENT),
                  pl.BlockSpec((D_MODEL, D_FF), fixed, **_RESIDENT),
                  pl.BlockSpec((D_FF, D_MODEL), fixed, **_RESIDENT),
                  pl.BlockSpec((1, D_MODEL), fixed, **_RESIDENT)],
        out_specs=pl.BlockSpec((tm, D_MODEL), row),
        out_shape=jax.ShapeDtypeStruct((n, D_MODEL), F32),
        compiler_params=_params(1),
        name="ffn",
    )(x, g, wg, wu, wd, fg)


def _proj_res_kernel(x_ref, y_ref, w_ref, o_ref):
    o_ref[...] = x_ref[...] + jnp.dot(y_ref[...], w_ref[...], preferred_element_type=F32)


def _proj_res(x, y, w):
    n, k = y.shape
    tm = min(TOKEN_TILE, n)
    row = lambda i: (i, 0)
    return pl.pallas_call(
        _proj_res_kernel,
        grid=(n // tm,),
        in_specs=[pl.BlockSpec((tm, D_MODEL), row),
                  pl.BlockSpec((tm, k), row),
                  pl.BlockSpec((k, D_MODEL), lambda i: (0, 0), **_RESIDENT)],
        out_specs=pl.BlockSpec((tm, D_MODEL), row),
        out_shape=jax.ShapeDtypeStruct((n, D_MODEL), F32),
        compiler_params=_params(1),
        name="proj_res",
    )(x, y, w)


def _ssd_in_kernel(x_ref, g_ref, wz_ref, wx_ref, wdt_ref, z_ref, xbc_ref, dt_ref):
    h = _rms(x_ref[...], g_ref[...]).astype(BF16)
    z_ref[...] = jnp.dot(h, wz_ref[...], preferred_element_type=F32).astype(BF16)
    xbc_ref[...] = jnp.dot(h, wx_ref[...], preferred_element_type=F32).astype(BF16)
    dt_ref[...] = jnp.dot(h, wdt_ref[...], preferred_element_type=F32)


def _ssd_in(x, g, wz, wx, wdt):
    n = x.shape[0]
    tm = min(TOKEN_TILE, n)
    row = lambda i: (i, 0)
    fixed = lambda i: (0, 0)
    return pl.pallas_call(
        _ssd_in_kernel,
        grid=(n // tm,),
        in_specs=[pl.BlockSpec((tm, D_MODEL), row),
                  pl.BlockSpec((1, D_MODEL), fixed, **_RESIDENT),
                  pl.BlockSpec((D_MODEL, D_INNER), fixed, **_RESIDENT),
                  pl.BlockSpec((D_MODEL, CONV_DIM), fixed, **_RESIDENT),
                  pl.BlockSpec((D_MODEL, LANES), fixed, **_RESIDENT)],
        out_specs=[pl.BlockSpec((tm, D_INNER), row),
                   pl.BlockSpec((tm, CONV_DIM), row),
                   pl.BlockSpec((tm, LANES), row)],
        out_shape=[jax.ShapeDtypeStruct((n, D_INNER), BF16),
                   jax.ShapeDtypeStruct((n, CONV_DIM), BF16),
                   jax.ShapeDtypeStruct((n, LANES), F32)],
        compiler_params=_params(1),
        name="ssd_in",
    )(x, g, wz, wx, wdt)


_Q_CS, _Q_ECS, _Q_WEND, _Q_ETOT = 0, 8, 16, 24
_N_COLQ = 32


def _ssd_kernel(x_ref, b_ref, c_ref, z_ref, dt_ref,
                cwx_ref, cwb_ref, cwc_ref, cbx_ref, cbb_ref, cbc_ref,
                dtb_ref, alog_ref, d_ref, ng_ref,
                o_ref,
                pad_ref, xs_ref, bt_ref, cs_ref, rowq_ref, colq_ref, sb_ref):
    T = x_ref.shape[1]
    Q = SSD_CHUNK
    nc = T // Q
    R = min(CONV_ROWS, T)
    H = CONV_HALO
    hp = HEADS_PER_GROUP
    P = SSM_HEAD_DIM

    def conv_slab(src_ref, c0, w_ref, bias_ref, store):
        zeros = jnp.zeros((H, LANES), F32)
        pad_ref[0:H, :] = zeros
        pad_ref[T + H:T + 2 * H, :] = zeros

        def fill(i, carry):
            r0 = pl.multiple_of(i * R, R)
            pad_ref[pl.ds(r0 + H, R), :] = src_ref[0, pl.ds(r0, R), c0:c0 + LANES].astype(F32)
            return carry

        lax.fori_loop(0, T // R, fill, 0)
        w = w_ref[:, c0:c0 + LANES]
        bias = bias_ref[:, c0:c0 + LANES]

        def blk(i, carry):
            r0 = pl.multiple_of(i * R, R)
            v = pad_ref[pl.ds(r0, R + 2 * H), :]
            acc = jnp.broadcast_to(bias, (R + 2 * H, LANES))
            for k in range(D_CONV):
                sh = (D_CONV // 2 - k) % (R + 2 * H)
                vs = v if sh == 0 else pltpu.roll(v, sh, 0)
                acc = acc + w[k:k + 1, :] * vs
            y = acc[H:R + H]
            store(r0, y * _sigmoid(y))
            return carry

        lax.fori_loop(0, T // R, blk, 0)

    for s in range(GROUP_DIM // LANES):
        def store_x(r0, y, s=s):
            xs_ref[pl.ds(r0, R), s * LANES:(s + 1) * LANES] = y
        conv_slab(x_ref, s * LANES, cwx_ref, cbx_ref, store_x)

    def store_b(r0, y):
        bt_ref[:, pl.ds(r0, R)] = y.T.astype(BF16)
    conv_slab(b_ref, 0, cwb_ref, cbb_ref, store_b)

    def store_c(r0, y):
        cs_ref[pl.ds(r0, R), :] = y.astype(BF16)
    conv_slab(c_ref, 0, cwc_ref, cbc_ref, store_c)

    raw = dt_ref[0, 0] + dtb_ref[0]
    dt = jnp.maximum(raw, 0.0) + jnp.log1p(jnp.exp(-jnp.abs(raw)))
    a = dt * (-jnp.exp(alog_ref[0]))
    lane = lax.broadcasted_iota(jnp.int32, (2 * hp, T), 1) % Q
    is_fwd = lax.broadcasted_iota(jnp.int32, (2 * hp, T), 0) < hp
    pre, suf = a, a
    s = 1
    while s < Q:
        pre = pre + jnp.where(lane >= s, pltpu.roll(pre, s, 1), 0.0)
        suf = suf + jnp.where(lane < Q - s, pltpu.roll(suf, T - s, 1), 0.0)
        s *= 2
    cs = jnp.where(is_fwd, pre, suf)
    rowq_ref[0:8, :] = cs
    rowq_ref[8:16, :] = dt
    rowq_ref[16:24, :] = jnp.exp(cs)
    rowq_ref[24:32, :] = jnp.exp(jnp.where(is_fwd, suf, pre) - a) * dt
    rowq_ref[32:40, :] = jnp.exp(pre + suf - a)

    def to_cols(c, carry):
        t0 = pl.multiple_of(c * Q, Q)
        tile = jnp.concatenate(
            [rowq_ref[0:8, pl.ds(t0, Q)], rowq_ref[16:40, pl.ds(t0, Q)],
             jnp.zeros((LANES - _N_COLQ, Q), F32)], axis=0)
        colq_ref[pl.ds(t0, Q), :] = tile.T
        return carry

    lax.fori_loop(0, nc, to_cols, 0)

    def col(colq, q, r, width):
        return jnp.broadcast_to(colq[:, q + r:q + r + 1], (Q, width))

    def per_head(colq, q):
        return jnp.concatenate([col(colq, q, r, P) for r in range(hp)], axis=1)

    def state_step(S, t0, colq, direction):
        off = direction * hp
        xw = (xs_ref[pl.ds(t0, Q), :] * per_head(colq, _Q_WEND + off)).astype(BF16)
        upd = jnp.dot(bt_ref[:, pl.ds(t0, Q)], xw, preferred_element_type=F32)
        return S * per_head(colq, _Q_ETOT + off) + upd

    def bwd_body(k, S):
        c = nc - 1 - k
        t0 = pl.multiple_of(c * Q, Q)
        sb_ref[c] = S.astype(BF16)
        return state_step(S, t0, colq_ref[pl.ds(t0, Q), :], 1)

    lax.fori_loop(0, nc, bwd_body, jnp.zeros((D_STATE, GROUP_DIM), F32))

    ii = lax.broadcasted_iota(jnp.int32, (Q, Q), 0)
    jj = lax.broadcasted_iota(jnp.int32, (Q, Q), 1)
    lower = ii >= jj
    strict = ii > jj
    diag = ii == jj
    dvec = d_ref[0]
    ng = ng_ref[0]

    def fwd_body(c, S):
        t0 = pl.multiple_of(c * Q, Q)
        cc = cs_ref[pl.ds(t0, Q), :]
        xc = xs_ref[pl.ds(t0, Q), :]
        colq = colq_ref[pl.ds(t0, Q), :]
        csr = rowq_ref[0:8, pl.ds(t0, Q)]
        dtr = rowq_ref[8:16, pl.ds(t0, Q)]
        cb = jnp.dot(cc, bt_ref[:, pl.ds(t0, Q)], preferred_element_type=F32)
        y_sf = jnp.dot(cc, S.astype(BF16), preferred_element_type=F32)
        y_sb = jnp.dot(cc, sb_ref[c], preferred_element_type=F32)
        ys = []
        for r in range(hp):
            arg = jnp.where(lower,
                            col(colq, _Q_CS, r, Q) - csr[r:r + 1, :],
                            col(colq, _Q_CS, hp + r, Q) - csr[hp + r:hp + r + 1, :])
            dtf = dtr[r:r + 1, :]
            dtb = dtr[hp + r:hp + r + 1, :]
            w = jnp.where(strict, dtf, jnp.where(diag, dtf + dtb, dtb))
            m = (cb * jnp.exp(arg) * w).astype(BF16)
            ys.append(jnp.dot(m, xc[:, r * P:(r + 1) * P].astype(BF16),
                              preferred_element_type=F32))
        y = jnp.concatenate(ys, axis=1)
        y = y + per_head(colq, _Q_ECS) * y_sf + per_head(colq, _Q_ECS + hp) * y_sb + dvec * xc
        zc = z_ref[0, pl.ds(t0, Q), :].astype(F32)
        y = y * (zc * _sigmoid(zc))
        o_ref[0, pl.ds(t0, Q), :] = _rms(y, ng).astype(BF16)
        return state_step(S, t0, colq, 0)

    lax.fori_loop(0, nc, fwd_body, jnp.zeros((D_STATE, GROUP_DIM), F32))


def _ssd(xbc, z, dt_rows, conv_w, conv_b, dt_bias, a_log, d_vec, norm_g):
    b, T, _ = xbc.shape
    nx = D_INNER // GROUP_DIM
    nb = D_INNER // D_STATE
    xblk = lambda i, g: (i, 0, g)
    bblk = lambda i, g: (i, 0, nb + g)
    cblk = lambda i, g: (i, 0, nb + SSM_GROUPS + g)
    per_g = lambda i, g: (g, 0, 0)
    return pl.pallas_call(
        _ssd_kernel,
        grid=(b, SSM_GROUPS),
        in_specs=[pl.BlockSpec((1, T, GROUP_DIM), xblk),
                  pl.BlockSpec((1, T, D_STATE), bblk),
                  pl.BlockSpec((1, T, D_STATE), cblk),
                  pl.BlockSpec((1, T, GROUP_DIM), xblk),
                  pl.BlockSpec((1, 1, 2 * HEADS_PER_GROUP, T), lambda i, g: (i, g, 0, 0)),
                  pl.BlockSpec((D_CONV, GROUP_DIM), lambda i, g: (0, g)),
                  pl.BlockSpec((D_CONV, D_STATE), lambda i, g: (0, nb + g)),
                  pl.BlockSpec((D_CONV, D_STATE), lambda i, g: (0, nb + SSM_GROUPS + g)),
                  pl.BlockSpec((1, GROUP_DIM), lambda i, g: (0, g)),
                  pl.BlockSpec((1, D_STATE), lambda i, g: (0, nb + g)),
                  pl.BlockSpec((1, D_STATE), lambda i, g: (0, nb + SSM_GROUPS + g)),
                  pl.BlockSpec((1, 2 * HEADS_PER_GROUP, 1), per_g),
                  pl.BlockSpec((1, 2 * HEADS_PER_GROUP, 1), per_g),
                  pl.BlockSpec((1, 1, GROUP_DIM), per_g),
                  pl.BlockSpec((1, 1, GROUP_DIM), per_g)],
        out_specs=pl.BlockSpec((1, T, GROUP_DIM), xblk),
        out_shape=jax.ShapeDtypeStruct((b, T, D_INNER), BF16),
        scratch_shapes=[pltpu.VMEM((T + 2 * CONV_HALO, LANES), F32),
                        pltpu.VMEM((T, GROUP_DIM), F32),
                        pltpu.VMEM((D_STATE, T), BF16),
                        pltpu.VMEM((T, D_STATE), BF16),
                        pltpu.VMEM((40, T), F32),
                        pltpu.VMEM((T, LANES), F32),
                        pltpu.VMEM((T // SSD_CHUNK, D_STATE, GROUP_DIM), BF16)],
        compiler_params=_params(2),
        name="ssd",
    )(xbc, xbc, xbc, z, dt_rows, conv_w, conv_w, conv_w, conv_b, conv_b, conv_b,
      dt_bias, a_log, d_vec, norm_g)


def _attn_in_kernel(x_ref, g_ref, w_ref, qg_ref, kg_ref, cos_ref, sin_ref, ones_ref,
                    q_ref, k_ref, vt_ref):
    h = _rms(x_ref[0], g_ref[...]).astype(BF16)
    qkv = jnp.dot(h, w_ref[...], preferred_element_type=F32)
    cos = cos_ref[...]
    sin = sin_ref[...]
    ones = ones_ref[...]
    tm = h.shape[0]
    first_half = (lax.broadcasted_iota(jnp.int32, (tm, LANES), 1) % AXIS_DIM) < AXIS_DIM // 2
    qd = N_HEADS * ATT_HEAD_DIM
    kd = N_KV_HEADS * ATT_HEAD_DIM

    def norm_rope(v, gain):
        sq = v * v
        hi = sq.astype(BF16)
        lo = (sq - hi.astype(F32)).astype(BF16)
        ss = (jnp.dot(hi, ones, preferred_element_type=F32)
              + jnp.dot(lo, ones, preferred_element_type=F32))
        n = v * lax.rsqrt(ss * (1.0 / ATT_HEAD_DIM) + EPS) * gain
        partner = jnp.where(first_half,
                            pltpu.roll(n, LANES - AXIS_DIM // 2, 1),
                            pltpu.roll(n, AXIS_DIM // 2, 1))
        return n * cos + partner * sin

    qgain = qg_ref[...] * (ATT_HEAD_DIM ** -0.5)
    for s in range(qd // LANES):
        q_ref[0, :, s * LANES:(s + 1) * LANES] = norm_rope(
            qkv[:, s * LANES:(s + 1) * LANES], qgain).astype(BF16)
    for s in range(kd // LANES):
        kk = norm_rope(qkv[:, qd + s * LANES:qd + (s + 1) * LANES], kg_ref[...]).astype(BF16)
        k_ref[0, 2 * s] = kk[:, :ATT_HEAD_DIM]
        k_ref[0, 2 * s + 1] = kk[:, ATT_HEAD_DIM:]
    vt_ref[0] = qkv[:, qd + kd:].T.astype(BF16)


def _attn_in(x, g, w, qg, kg, cos, sin, ones):
    b, T, _ = x.shape
    tm = min(TOKEN_TILE, T)
    qkv_dim = (N_HEADS + 2 * N_KV_HEADS) * ATT_HEAD_DIM
    fixed = lambda i, t: (0, 0)
    return pl.pallas_call(
        _attn_in_kernel,
        grid=(b, T // tm),
        in_specs=[pl.BlockSpec((1, tm, D_MODEL), lambda i, t: (i, t, 0)),
                  pl.BlockSpec((1, D_MODEL), fixed, **_RESIDENT),
                  pl.BlockSpec((D_MODEL, qkv_dim), fixed, **_RESIDENT),
                  pl.BlockSpec((1, LANES), fixed, **_RESIDENT),
                  pl.BlockSpec((1, LANES), fixed, **_RESIDENT),
                  pl.BlockSpec((tm, LANES), lambda i, t: (t, 0)),
                  pl.BlockSpec((tm, LANES), lambda i, t: (t, 0)),
                  pl.BlockSpec((LANES, LANES), fixed, **_RESIDENT)],
        out_specs=[pl.BlockSpec((1, tm, N_HEADS * ATT_HEAD_DIM), lambda i, t: (i, t, 0)),
                   pl.BlockSpec((1, N_KV_HEADS, tm, ATT_HEAD_DIM), lambda i, t: (i, 0, t, 0)),
                   pl.BlockSpec((1, N_KV_HEADS * ATT_HEAD_DIM, tm), lambda i, t: (i, 0, t))],
        out_shape=[jax.ShapeDtypeStruct((b, T, N_HEADS * ATT_HEAD_DIM), BF16),
                   jax.ShapeDtypeStruct((b, N_KV_HEADS, T, ATT_HEAD_DIM), BF16),
                   jax.ShapeDtypeStruct((b, N_KV_HEADS * ATT_HEAD_DIM, T), BF16)],
        compiler_params=_params(2),
        name="attn_in",
    )(x, g, w, qg, kg, cos, sin, ones)


def _flash_kernel(q_ref, k_ref, vt_ref, o_ref):
    tq = q_ref.shape[1]
    T = k_ref.shape[2]
    tk = min(ATT_K_TILE, T)
    dh = ATT_HEAD_DIM
    ones = jnp.ones((2 * SUBLANES, tk), BF16)
    for g in range(N_KV_HEADS):
        outs = []
        for r in range(KV_REP):
            hd = g * KV_REP + r
            q = q_ref[0, :, hd * dh:(hd + 1) * dh]

            def body(c, carry, q=q, g=g):
                m, acc = carry
                k0 = pl.multiple_of(c * tk, tk)
                kc = k_ref[0, g, pl.ds(k0, tk), :]
                s = lax.dot_general(kc, q, (((1,), (1,)), ((), ())),
                                    preferred_element_type=F32)
                m_new = jnp.maximum(m, jnp.max(s, axis=0, keepdims=True))
                p = jnp.exp(s - m_new).astype(BF16)
                va = jnp.concatenate([vt_ref[0, g * dh:(g + 1) * dh, pl.ds(k0, tk)], ones], axis=0)
                acc = jnp.exp(m - m_new) * acc + jnp.dot(va, p, preferred_element_type=F32)
                return m_new, acc

            m0 = jnp.full((1, tq), -jnp.inf, F32)
            acc0 = jnp.zeros((dh + 2 * SUBLANES, tq), F32)
            _, acc = lax.fori_loop(0, T // tk, body, (m0, acc0))
            outs.append(acc[:dh] / acc[dh:dh + 1])
        ot = jnp.concatenate(outs, axis=0)
        o_ref[0, :, g * KV_REP * dh:(g + 1) * KV_REP * dh] = ot.T.astype(BF16)


def _flash(q, k, vt):
    b, T, qd = q.shape
    tq = min(ATT_Q_TILE, T)
    return pl.pallas_call(
        _flash_kernel,
        grid=(b, T // tq),
        in_specs=[pl.BlockSpec((1, tq, qd), lambda i, t: (i, t, 0)),
                  pl.BlockSpec((1, N_KV_HEADS, T, ATT_HEAD_DIM), lambda i, t: (i, 0, 0, 0)),
                  pl.BlockSpec((1, N_KV_HEADS * ATT_HEAD_DIM, T), lambda i, t: (i, 0, 0))],
        out_specs=pl.BlockSpec((1, tq, qd), lambda i, t: (i, t, 0)),
        out_shape=jax.ShapeDtypeStruct((b, T, qd), BF16),
        compiler_params=_params(2),
        name="flash",
    )(q, k, vt)


def _rope_tables(T):
    t = jnp.arange(T, dtype=jnp.int32)
    pos = jnp.stack([(t // GRID_W).astype(F32), (t % GRID_W).astype(F32)], axis=1)
    inv_freq = ROPE_THETA ** (-jnp.arange(0, AXIS_DIM, 2, dtype=F32) / AXIS_DIM)
    ang = pos[:, :, None] * inv_freq[None, None, :]
    cos = jnp.cos(ang)
    sin = jnp.sin(ang)
    cos_d = jnp.concatenate([cos, cos], axis=-1).reshape(T, ATT_HEAD_DIM)
    sin_d = jnp.concatenate([-sin, sin], axis=-1).reshape(T, ATT_HEAD_DIM)
    return jnp.tile(cos_d, (1, LANES // ATT_HEAD_DIM)), jnp.tile(sin_d, (1, LANES // ATT_HEAD_DIM))


def _group_order(v):
    return v.reshape(2, SSM_GROUPS, HEADS_PER_GROUP).transpose(1, 0, 2).reshape(SSM_GROUPS, 2 * HEADS_PER_GROUP)


def kernel(x_prompt, x_sample, norm_g, ffn_w_gate, ffn_w_up, ffn_w_down, ssm_w_in, ssm_conv_w,
           ssm_conv_b, ssm_dt_bias, ssm_A_log, ssm_D, ssm_norm_g, ssm_w_out, attn_w_qkv,
           attn_q_norm, attn_k_norm, attn_w_out, final_norm):
    nb_prompt = x_prompt.shape[0]
    x = jnp.concatenate([x_prompt, x_sample], axis=0)
    b, T, _ = x.shape
    n = b * T
    x = x.reshape(n, D_MODEL)
    depth = norm_g.shape[0]

    wg = ffn_w_gate.astype(BF16)
    wu = ffn_w_up.astype(BF16)
    wd = ffn_w_down.astype(BF16)
    fg = final_norm.reshape(1, D_MODEL)
    cos, sin = _rope_tables(T)
    half = jnp.arange(LANES) // ATT_HEAD_DIM
    head_ones = (half[:, None] == half[None, :]).astype(BF16)

    for i in range(depth):
        j = i // 2
        x = _ffn(x, norm_g[i, 0].reshape(1, D_MODEL), wg[i, 0], wu[i, 0], wd[i, 0], fg, False)
        g_mix = norm_g[i, 1].reshape(1, D_MODEL)
        if i % 2 == 0:
            w_in = ssm_w_in[j]
            wz = w_in[:, :D_INNER].astype(BF16)
            wx = w_in[:, D_INNER:D_INNER + CONV_DIM].astype(BF16)
            wdt = w_in[:, D_INNER + CONV_DIM:].reshape(D_MODEL, 2, SSM_GROUPS, HEADS_PER_GROUP)
            wdt = wdt.transpose(0, 2, 1, 3).reshape(D_MODEL, 2 * SSM_HEADS)
            wdt = jnp.pad(wdt, ((0, 0), (0, LANES - 2 * SSM_HEADS))).astype(BF16)
            z, xbc, dt_raw = _ssd_in(x, g_mix, wz, wx, wdt)
            dt_rows = dt_raw[:, :2 * SSM_HEADS].reshape(b, T, SSM_GROUPS, 2 * HEADS_PER_GROUP)
            dt_rows = dt_rows.transpose(0, 2, 3, 1)
            y = _ssd(xbc.reshape(b, T, CONV_DIM), z.reshape(b, T, D_INNER), dt_rows,
                     ssm_conv_w[j], ssm_conv_b[j].reshape(1, CONV_DIM),
                     _group_order(ssm_dt_bias[j])[:, :, None],
                     _group_order(ssm_A_log[j])[:, :, None],
                     jnp.repeat(ssm_D[j], SSM_HEAD_DIM).reshape(SSM_GROUPS, 1, GROUP_DIM),
                     ssm_norm_g[j].reshape(SSM_GROUPS, 1, GROUP_DIM))
            x = _proj_res(x, y.reshape(n, D_INNER), ssm_w_out[j].astype(BF16))
        else:
            q, k, vt = _attn_in(x.reshape(b, T, D_MODEL), g_mix, attn_w_qkv[j].astype(BF16),
                                jnp.tile(attn_q_norm[j], LANES // ATT_HEAD_DIM).reshape(1, LANES),
                                jnp.tile(attn_k_norm[j], LANES // ATT_HEAD_DIM).reshape(1, LANES),
                                cos, sin, head_ones)
            o = _flash(q, k, vt)
            x = _proj_res(x, o.reshape(n, N_HEADS * ATT_HEAD_DIM), attn_w_out[j].astype(BF16))
        x = _ffn(x, norm_g[i, 2].reshape(1, D_MODEL), wg[i, 1], wu[i, 1], wd[i, 1], fg,
                 i == depth - 1)
    x = x.reshape(b, T, D_MODEL)
    return (x[:nb_prompt], x[nb_prompt:])
```

```python
import functools
import math

import jax
import jax.numpy as jnp
from jax import lax
from jax.experimental import pallas as pl
from jax.experimental.pallas import tpu as pltpu

F32 = jnp.float32
BF16 = jnp.bfloat16

D_MODEL = 1024
GRID_W = 64
D_FF = 2816
D_INNER = 2048
SSM_HEAD_DIM = 64
SSM_HEADS = 32
SSM_GROUPS = 8
HEADS_PER_GROUP = 4
GROUP_DIM = HEADS_PER_GROUP * SSM_HEAD_DIM
D_STATE = 128
D_CONV = 5
CONV_DIM = D_INNER + 2 * SSM_GROUPS * D_STATE
ATT_HEAD_DIM = 64
N_HEADS = 16
N_KV_HEADS = 4
KV_REP = 4
AXIS_DIM = 32
ROPE_THETA = 10000.0
EPS = 1e-6

LANES = 128
SUBLANES = 8
VMEM_LIMIT = 56 * 1024 * 1024

TOKEN_TILE = 512
SSD_CHUNK = 128
CONV_ROWS = 256
CONV_HALO = 8
ATT_Q_TILE = 256
ATT_K_TILE = 512
FF_SPLITS = (1024, 1024, 768)

_RESIDENT = dict(pipeline_mode=pl.Buffered(1))


def _params(n_axes):
    return pltpu.CompilerParams(dimension_semantics=("parallel",) * n_axes,
                                vmem_limit_bytes=VMEM_LIMIT)


def _sigmoid(x):
    return 1.0 / (1.0 + jnp.exp(-x))


def _rms(x, g):
    return x * lax.rsqrt(jnp.mean(x * x, axis=-1, keepdims=True) + EPS) * g


def _ffn_kernel(x_ref, g_ref, wg_ref, wu_ref, wd_ref, fg_ref, o_ref, *, final):
    x = x_ref[...]
    h = _rms(x, g_ref[...]).astype(BF16)
    acc = None
    c0 = 0
    for cw in FF_SPLITS:
        gate = jnp.dot(h, wg_ref[:, c0:c0 + cw], preferred_element_type=F32)
        up = jnp.dot(h, wu_ref[:, c0:c0 + cw], preferred_element_type=F32)
        a = (gate * _sigmoid(gate) * up).astype(BF16)
        part = jnp.dot(a, wd_ref[c0:c0 + cw, :], preferred_element_type=F32)
        acc = part if acc is None else acc + part
        c0 += cw
    y = x + 0.5 * acc
    if final:
        y = _rms(y, fg_ref[...])
    o_ref[...] = y


def _ffn(x, g, wg, wu, wd, fg, final):
    n = x.shape[0]
    tm = min(TOKEN_TILE, n)
    row = lambda i: (i, 0)
    fixed = lambda i: (0, 0)
    return pl.pallas_call(
        functools.partial(_ffn_kernel, final=final),
        grid=(n // tm,),
        in_specs=[pl.BlockSpec((tm, D_MODEL), row),
                  pl.BlockSpec((1, D_MODEL), fixed, **_RESIDENT),
                  pl.BlockSpec((D_MODEL, D_FF), fixed, **_RESIDENT),
                  pl.BlockSpec((D_MODEL, D_FF), fixed, **_RESIDENT),
                  pl.BlockSpec((D_FF, D_MODEL), fixed, **_RESIDENT),
                  pl.BlockSpec((1, D_MODEL), fixed, **_RESIDENT)],
        out_specs=pl.BlockSpec((tm, D_MODEL), row),
        out_shape=jax.ShapeDtypeStruct((n, D_MODEL), F32),
        compiler_params=_params(1),
        name="ffn",
    )(x, g, wg, wu, wd, fg)


def _proj_res_kernel(x_ref, y_ref, w_ref, o_ref):
    o_ref[...] = x_ref[...] + jnp.dot(y_ref[...], w_ref[...], preferred_element_type=F32)


def _proj_res(x, y, w):
    n, k = y.shape
    tm = min(TOKEN_TILE, n)
    row = lambda i: (i, 0)
    return pl.pallas_call(
        _proj_res_kernel,
        grid=(n // tm,),
        in_specs=[pl.BlockSpec((tm, D_MODEL), row),
                  pl.BlockSpec((tm, k), row),
                  pl.BlockSpec((k, D_MODEL), lambda i: (0, 0), **_RESIDENT)],
        out_specs=pl.BlockSpec((tm, D_MODEL), row),
        out_shape=jax.ShapeDtypeStruct((n, D_MODEL), F32),
        compiler_params=_params(1),
        name="proj_res",
    )(x, y, w)


def _ssd_in_kernel(x_ref, g_ref, wz_ref, wx_ref, wdt_ref, z_ref, xbc_ref, dt_ref):
    h = _rms(x_ref[...], g_ref[...]).astype(BF16)
    z_ref[...] = jnp.dot(h, wz_ref[...], preferred_element_type=F32).astype(BF16)
    xbc_ref[...] = jnp.dot(h, wx_ref[...], preferred_element_type=F32).astype(BF16)
    dt_ref[...] = jnp.dot(h, wdt_ref[...], preferred_element_type=F32)


def _ssd_in(x, g, wz, wx, wdt):
    n = x.shape[0]
    tm = min(TOKEN_TILE, n)
    row = lambda i: (i, 0)
    fixed = lambda i: (0, 0)
    return pl.pallas_call(
        _ssd_in_kernel,
        grid=(n // tm,),
        in_specs=[pl.BlockSpec((tm, D_MODEL), row),
                  pl.BlockSpec((1, D_MODEL), fixed, **_RESIDENT),
                  pl.BlockSpec((D_MODEL, D_INNER), fixed, **_RESIDENT),
                  pl.BlockSpec((D_MODEL, CONV_DIM), fixed, **_RESIDENT),
                  pl.BlockSpec((D_MODEL, LANES), fixed, **_RESIDENT)],
        out_specs=[pl.BlockSpec((tm, D_INNER), row),
                   pl.BlockSpec((tm, CONV_DIM), row),
                   pl.BlockSpec((tm, LANES), row)],
        out_shape=[jax.ShapeDtypeStruct((n, D_INNER), BF16),
                   jax.ShapeDtypeStruct((n, CONV_DIM), BF16),
                   jax.ShapeDtypeStruct((n, LANES), F32)],
        compiler_params=_params(1),
        name="ssd_in",
    )(x, g, wz, wx, wdt)


_Q_CS, _Q_ECS, _Q_WEND, _Q_ETOT = 0, 8, 16, 24
_N_COLQ = 32


def _ssd_kernel(x_ref, b_ref, c_ref, z_ref, dt_ref,
                cwx_ref, cwb_ref, cwc_ref, cbx_ref, cbb_ref, cbc_ref,
                dtb_ref, alog_ref, d_ref, ng_ref,
                o_ref,
                pad_ref, xs_ref, bt_ref, cs_ref, rowq_ref, colq_ref, sb_ref):
    T = x_ref.shape[1]
    Q = SSD_CHUNK
    nc = T // Q
    R = min(CONV_ROWS, T)
    H = CONV_HALO
    hp = HEADS_PER_GROUP
    P = SSM_HEAD_DIM

    def conv_slab(src_ref, c0, w_ref, bias_ref, store):
        zeros = jnp.zeros((H, LANES), F32)
        pad_ref[0:H, :] = zeros
        pad_ref[T + H:T + 2 * H, :] = zeros

        def fill(i, carry):
            r0 = pl.multiple_of(i * R, R)
            pad_ref[pl.ds(r0 + H, R), :] = src_ref[0, pl.ds(r0, R), c0:c0 + LANES].astype(F32)
            return carry

        lax.fori_loop(0, T // R, fill, 0)
        w = w_ref[:, c0:c0 + LANES]
        bias = bias_ref[:, c0:c0 + LANES]

        def blk(i, carry):
            r0 = pl.multiple_of(i * R, R)
            v = pad_ref[pl.ds(r0, R + 2 * H), :]
            acc = jnp.broadcast_to(bias, (R + 2 * H, LANES))
            for k in range(D_CONV):
                sh = (D_CONV // 2 - k) % (R + 2 * H)
                vs = v if sh == 0 else pltpu.roll(v, sh, 0)
                acc = acc + w[k:k + 1, :] * vs
            y = acc[H:R + H]
            store(r0, y * _sigmoid(y))
            return carry

        lax.fori_loop(0, T // R, blk, 0)

    for s in range(GROUP_DIM // LANES):
        def store_x(r0, y, s=s):
            xs_ref[pl.ds(r0, R), s * LANES:(s + 1) * LANES] = y
        conv_slab(x_ref, s * LANES, cwx_ref, cbx_ref, store_x)

    def store_b(r0, y):
        bt_ref[:, pl.ds(r0, R)] = y.T.astype(BF16)
    conv_slab(b_ref, 0, cwb_ref, cbb_ref, store_b)

    def store_c(r0, y):
        cs_ref[pl.ds(r0, R), :] = y.astype(BF16)
    conv_slab(c_ref, 0, cwc_ref, cbc_ref, store_c)

    raw = dt_ref[0, 0] + dtb_ref[0]
    dt = jnp.maximum(raw, 0.0) + jnp.log1p(jnp.exp(-jnp.abs(raw)))
    a = dt * (-jnp.exp(alog_ref[0]))
    lane = lax.broadcasted_iota(jnp.int32, (2 * hp, T), 1) % Q
    is_fwd = lax.broadcasted_iota(jnp.int32, (2 * hp, T), 0) < hp
    pre, suf = a, a
    s = 1
    while s < Q:
        pre = pre + jnp.where(lane >= s, pltpu.roll(pre, s, 1), 0.0)
        suf = suf + jnp.where(lane < Q - s, pltpu.roll(suf, T - s, 1), 0.0)
        s *= 2
    cs = jnp.where(is_fwd, pre, suf)
    rowq_ref[0:8, :] = cs
    rowq_ref[8:16, :] = dt
    rowq_ref[16:24, :] = jnp.exp(cs)
    rowq_ref[24:32, :] = jnp.exp(jnp.where(is_fwd, suf, pre) - a) * dt
    rowq_ref[32:40, :] = jnp.exp(pre + suf - a)

    def to_cols(c, carry):
        t0 = pl.multiple_of(c * Q, Q)
        tile = jnp.concatenate(
            [rowq_ref[0:8, pl.ds(t0, Q)], rowq_ref[16:40, pl.ds(t0, Q)],
             jnp.zeros((LANES - _N_COLQ, Q), F32)], axis=0)
        colq_ref[pl.ds(t0, Q), :] = tile.T
        return carry

    lax.fori_loop(0, nc, to_cols, 0)

    def col(colq, q, r, width):
        return jnp.broadcast_to(colq[:, q + r:q + r + 1], (Q, width))

    def per_head(colq, q):
        return jnp.concatenate([col(colq, q, r, P) for r in range(hp)], axis=1)

    def state_step(S, t0, colq, direction):
        off = direction * hp
        xw = (xs_ref[pl.ds(t0, Q), :] * per_head(colq, _Q_WEND + off)).astype(BF16)
        upd = jnp.dot(bt_ref[:, pl.ds(t0, Q)], xw, preferred_element_type=F32)
        return S * per_head(colq, _Q_ETOT + off) + upd

    def bwd_body(k, S):
        c = nc - 1 - k
        t0 = pl.multiple_of(c * Q, Q)
        sb_ref[c] = S.astype(BF16)
        return state_step(S, t0, colq_ref[pl.ds(t0, Q), :], 1)

    lax.fori_loop(0, nc, bwd_body, jnp.zeros((D_STATE, GROUP_DIM), F32))

    ii = lax.broadcasted_iota(jnp.int32, (Q, Q), 0)
    jj = lax.broadcasted_iota(jnp.int32, (Q, Q), 1)
    lower = ii >= jj
    strict = ii > jj
    diag = ii == jj
    dvec = d_ref[0]
    ng = ng_ref[0]

    def fwd_body(c, S):
        t0 = pl.multiple_of(c * Q, Q)
        cc = cs_ref[pl.ds(t0, Q), :]
        xc = xs_ref[pl.ds(t0, Q), :]
        colq = colq_ref[pl.ds(t0, Q), :]
        csr = rowq_ref[0:8, pl.ds(t0, Q)]
        dtr = rowq_ref[8:16, pl.ds(t0, Q)]
        cb = jnp.dot(cc, bt_ref[:, pl.ds(t0, Q)], preferred_element_type=F32)
        y_sf = jnp.dot(cc, S.astype(BF16), preferred_element_type=F32)
        y_sb = jnp.dot(cc, sb_ref[c], preferred_element_type=F32)
        ys = []
        for r in range(hp):
            arg = jnp.where(lower,
                            col(colq, _Q_CS, r, Q) - csr[r:r + 1, :],
                            col(colq, _Q_CS, hp + r, Q) - csr[hp + r:hp + r + 1, :])
            dtf = dtr[r:r + 1, :]
            dtb = dtr[hp + r:hp + r + 1, :]
            w = jnp.where(strict, dtf, jnp.where(diag, dtf + dtb, dtb))
            m = (cb * jnp.exp(arg) * w).astype(BF16)
            ys.append(jnp.dot(m, xc[:, r * P:(r + 1) * P].astype(BF16),
                              preferred_element_type=F32))
        y = jnp.concatenate(ys, axis=1)
        y = y + per_head(colq, _Q_ECS) * y_sf + per_head(colq, _Q_ECS + hp) * y_sb + dvec * xc
        zc = z_ref[0, pl.ds(t0, Q), :].astype(F32)
        y = y * (zc * _sigmoid(zc))
        o_ref[0, pl.ds(t0, Q), :] = _rms(y, ng).astype(BF16)
        return state_step(S, t0, colq, 0)

    lax.fori_loop(0, nc, fwd_body, jnp.zeros((D_STATE, GROUP_DIM), F32))


def _ssd(xbc, z, dt_rows, conv_w, conv_b, dt_bias, a_log, d_vec, norm_g):
    b, T, _ = xbc.shape
    nx = D_INNER // GROUP_DIM
    nb = D_INNER // D_STATE
    xblk = lambda i, g: (i, 0, g)
    bblk = lambda i, g: (i, 0, nb + g)
    cblk = lambda i, g: (i, 0, nb + SSM_GROUPS + g)
    per_g = lambda i, g: (g, 0, 0)
    return pl.pallas_call(
        _ssd_kernel,
        grid=(b, SSM_GROUPS),
        in_specs=[pl.BlockSpec((1, T, GROUP_DIM), xblk),
                  pl.BlockSpec((1, T, D_STATE), bblk),
                  pl.BlockSpec((1, T, D_STATE), cblk),
                  pl.BlockSpec((1, T, GROUP_DIM), xblk),
                  pl.BlockSpec((1, 1, 2 * HEADS_PER_GROUP, T), lambda i, g: (i, g, 0, 0)),
                  pl.BlockSpec((D_CONV, GROUP_DIM), lambda i, g: (0, g)),
                  pl.BlockSpec((D_CONV, D_STATE), lambda i, g: (0, nb + g)),
                  pl.BlockSpec((D_CONV, D_STATE), lambda i, g: (0, nb + SSM_GROUPS + g)),
                  pl.BlockSpec((1, GROUP_DIM), lambda i, g: (0, g)),
                  pl.BlockSpec((1, D_STATE), lambda i, g: (0, nb + g)),
                  pl.BlockSpec((1, D_STATE), lambda i, g: (0, nb + SSM_GROUPS + g)),
                  pl.BlockSpec((1, 2 * HEADS_PER_GROUP, 1), per_g),
                  pl.BlockSpec((1, 2 * HEADS_PER_GROUP, 1), per_g),
                  pl.BlockSpec((1, 1, GROUP_DIM), per_g),
                  pl.BlockSpec((1, 1, GROUP_DIM), per_g)],
        out_specs=pl.BlockSpec((1, T, GROUP_DIM), xblk),
        out_shape=jax.ShapeDtypeStruct((b, T, D_INNER), BF16),
        scratch_shapes=[pltpu.VMEM((T + 2 * CONV_HALO, LANES), F32),
                        pltpu.VMEM((T, GROUP_DIM), F32),
                        pltpu.VMEM((D_STATE, T), BF16),
                        pltpu.VMEM((T, D_STATE), BF16),
                        pltpu.VMEM((40, T), F32),
                        pltpu.VMEM((T, LANES), F32),
                        pltpu.VMEM((T // SSD_CHUNK, D_STATE, GROUP_DIM), BF16)],
        compiler_params=_params(2),
        name="ssd",
    )(xbc, xbc, xbc, z, dt_rows, conv_w, conv_w, conv_w, conv_b, conv_b, conv_b,
      dt_bias, a_log, d_vec, norm_g)


def _attn_in_kernel(x_ref, g_ref, w_ref, qg_ref, kg_ref, cos_ref, sin_ref, ones_ref,
                    q_ref, k_ref, vt_ref):
    h = _rms(x_ref[0], g_ref[...]).astype(BF16)
    qkv = jnp.dot(h, w_ref[...], preferred_element_type=F32)
    cos = cos_ref[...]
    sin = sin_ref[...]
    ones = ones_ref[...]
    tm = h.shape[0]
    first_half = (lax.broadcasted_iota(jnp.int32, (tm, LANES), 1) % AXIS_DIM) < AXIS_DIM // 2
    qd = N_HEADS * ATT_HEAD_DIM
    kd = N_KV_HEADS * ATT_HEAD_DIM

    def norm_rope(v, gain):
        sq = v * v
        hi = sq.astype(BF16)
        lo = (sq - hi.astype(F32)).astype(BF16)
        ss = (jnp.dot(hi, ones, preferred_element_type=F32)
              + jnp.dot(lo, ones, preferred_element_type=F32))
        n = v * lax.rsqrt(ss * (1.0 / ATT_HEAD_DIM) + EPS) * gain
        partner = jnp.where(first_half,
                            pltpu.roll(n, LANES - AXIS_DIM // 2, 1),
                            pltpu.roll(n, AXIS_DIM // 2, 1))
        return n * cos + partner * sin

    qgain = qg_ref[...] * (ATT_HEAD_DIM ** -0.5 * math.log2(math.e))
    for s in range(qd // LANES):
        q_ref[0, :, s * LANES:(s + 1) * LANES] = norm_rope(
            qkv[:, s * LANES:(s + 1) * LANES], qgain).astype(BF16)
    for s in range(kd // LANES):
        kk = norm_rope(qkv[:, qd + s * LANES:qd + (s + 1) * LANES], kg_ref[...]).astype(BF16)
        k_ref[0, 2 * s] = kk[:, :ATT_HEAD_DIM]
        k_ref[0, 2 * s + 1] = kk[:, ATT_HEAD_DIM:]
    vt_ref[0] = qkv[:, qd + kd:].T.astype(BF16)


def _attn_in(x, g, w, qg, kg, cos, sin, ones):
    b, T, _ = x.shape
    tm = min(TOKEN_TILE, T)
    qkv_dim = (N_HEADS + 2 * N_KV_HEADS) * ATT_HEAD_DIM
    fixed = lambda i, t: (0, 0)
    return pl.pallas_call(
        _attn_in_kernel,
        grid=(b, T // tm),
        in_specs=[pl.BlockSpec((1, tm, D_MODEL), lambda i, t: (i, t, 0)),
                  pl.BlockSpec((1, D_MODEL), fixed, **_RESIDENT),
                  pl.BlockSpec((D_MODEL, qkv_dim), fixed, **_RESIDENT),
                  pl.BlockSpec((1, LANES), fixed, **_RESIDENT),
                  pl.BlockSpec((1, LANES), fixed, **_RESIDENT),
                  pl.BlockSpec((tm, LANES), lambda i, t: (t, 0)),
                  pl.BlockSpec((tm, LANES), lambda i, t: (t, 0)),
                  pl.BlockSpec((LANES, LANES), fixed, **_RESIDENT)],
        out_specs=[pl.BlockSpec((1, tm, N_HEADS * ATT_HEAD_DIM), lambda i, t: (i, t, 0)),
                   pl.BlockSpec((1, N_KV_HEADS, tm, ATT_HEAD_DIM), lambda i, t: (i, 0, t, 0)),
                   pl.BlockSpec((1, N_KV_HEADS * ATT_HEAD_DIM, tm), lambda i, t: (i, 0, t))],
        out_shape=[jax.ShapeDtypeStruct((b, T, N_HEADS * ATT_HEAD_DIM), BF16),
                   jax.ShapeDtypeStruct((b, N_KV_HEADS, T, ATT_HEAD_DIM), BF16),
                   jax.ShapeDtypeStruct((b, N_KV_HEADS * ATT_HEAD_DIM, T), BF16)],
        compiler_params=_params(2),
        name="attn_in",
    )(x, g, w, qg, kg, cos, sin, ones)


def _flash_kernel(q_ref, k_ref, vt_ref, o_ref, s_ref):
    tq = q_ref.shape[1]
    T = k_ref.shape[2]
    tk = s_ref.shape[1]
    dh = ATT_HEAD_DIM
    ones = jnp.ones((2 * SUBLANES, tk), BF16)
    nq = KV_REP * tq
    nk = T // tk

    def scores(g, c, slot):
        q4 = jnp.concatenate([q_ref[0, :, (g * KV_REP + r) * dh:(g * KV_REP + r + 1) * dh]
                              for r in range(KV_REP)], axis=0)
        k0 = pl.multiple_of(c * tk, tk)
        kc = k_ref[0, g, pl.ds(k0, tk), :]
        s_ref[slot] = lax.dot_general(kc, q4, (((1,), (1,)), ((), ())),
                                      preferred_element_type=F32)

    def attend(g, c, slot, m, acc):
        k0 = pl.multiple_of(c * tk, tk)
        va = jnp.concatenate([vt_ref[0, g * dh:(g + 1) * dh, pl.ds(k0, tk)], ones], axis=0)
        s = s_ref[slot]
        m_new = jnp.maximum(m, jnp.max(s, axis=0, keepdims=True))
        p = jnp.exp2(s - m_new).astype(BF16)
        acc = jnp.exp2(m - m_new) * acc + jnp.dot(va, p, preferred_element_type=F32)
        return m_new, acc

    scores(0, 0, 0)
    for g in range(N_KV_HEADS):

        def body(i, carry, g=g):
            m, acc = carry
            c = 2 * i
            scores(g, c + 1, 1)
            m, acc = attend(g, c, 0, m, acc)
            scores(g, c + 2, 0)
            return attend(g, c + 1, 1, m, acc)

        init = (jnp.full((1, nq), -jnp.inf, F32), jnp.zeros((dh + 2 * SUBLANES, nq), F32))
        m, acc = lax.fori_loop(0, nk // 2 - 1, body, init)
        scores(g, nk - 1, 1)
        m, acc = attend(g, nk - 2, 0, m, acc)
        if g + 1 < N_KV_HEADS:
            scores(g + 1, 0, 0)
        _, acc = attend(g, nk - 1, 1, m, acc)
        o4 = acc[:dh] / acc[dh:dh + 1]
        ot = jnp.concatenate([o4[:, r * tq:(r + 1) * tq] for r in range(KV_REP)], axis=0)
        o_ref[0, :, g * KV_REP * dh:(g + 1) * KV_REP * dh] = ot.T.astype(BF16)


def _flash(q, k, vt):
    b, T, qd = q.shape
    tq = min(ATT_Q_TILE, T)
    tk = min(ATT_K_TILE, T // 2)
    return pl.pallas_call(
        _flash_kernel,
        grid=(b, T // tq),
        in_specs=[pl.BlockSpec((1, tq, qd), lambda i, t: (i, t, 0)),
                  pl.BlockSpec((1, N_KV_HEADS, T, ATT_HEAD_DIM), lambda i, t: (i, 0, 0, 0)),
                  pl.BlockSpec((1, N_KV_HEADS * ATT_HEAD_DIM, T), lambda i, t: (i, 0, 0))],
        out_specs=pl.BlockSpec((1, tq, qd), lambda i, t: (i, t, 0)),
        out_shape=jax.ShapeDtypeStruct((b, T, qd), BF16),
        scratch_shapes=[pltpu.VMEM((2, tk, KV_REP * tq), F32)],
        compiler_params=_params(2),
        name="flash",
    )(q, k, vt)


def _rope_tables(T):
    t = jnp.arange(T, dtype=jnp.int32)
    pos = jnp.stack([(t // GRID_W).astype(F32), (t % GRID_W).astype(F32)], axis=1)
    inv_freq = ROPE_THETA ** (-jnp.arange(0, AXIS_DIM, 2, dtype=F32) / AXIS_DIM)
    ang = pos[:, :, None] * inv_freq[None, None, :]
    cos = jnp.cos(ang)
    sin = jnp.sin(ang)
    cos_d = jnp.concatenate([cos, cos], axis=-1).reshape(T, ATT_HEAD_DIM)
    sin_d = jnp.concatenate([-sin, sin], axis=-1).reshape(T, ATT_HEAD_DIM)
    return jnp.tile(cos_d, (1, LANES // ATT_HEAD_DIM)), jnp.tile(sin_d, (1, LANES // ATT_HEAD_DIM))


def _group_order(v):
    return v.reshape(2, SSM_GROUPS, HEADS_PER_GROUP).transpose(1, 0, 2).reshape(SSM_GROUPS, 2 * HEADS_PER_GROUP)


def kernel(x_prompt, x_sample, norm_g, ffn_w_gate, ffn_w_up, ffn_w_down, ssm_w_in, ssm_conv_w,
           ssm_conv_b, ssm_dt_bias, ssm_A_log, ssm_D, ssm_norm_g, ssm_w_out, attn_w_qkv,
           attn_q_norm, attn_k_norm, attn_w_out, final_norm):
    nb_prompt = x_prompt.shape[0]
    x = jnp.concatenate([x_prompt, x_sample], axis=0)
    b, T, _ = x.shape
    n = b * T
    x = x.reshape(n, D_MODEL)
    depth = norm_g.shape[0]

    wg = ffn_w_gate.astype(BF16)
    wu = ffn_w_up.astype(BF16)
    wd = ffn_w_down.astype(BF16)
    fg = final_norm.reshape(1, D_MODEL)
    cos, sin = _rope_tables(T)
    half = jnp.arange(LANES) // ATT_HEAD_DIM
    head_ones = (half[:, None] == half[None, :]).astype(BF16)

    for i in range(depth):
        j = i // 2
        x = _ffn(x, norm_g[i, 0].reshape(1, D_MODEL), wg[i, 0], wu[i, 0], wd[i, 0], fg, False)
        g_mix = norm_g[i, 1].reshape(1, D_MODEL)
        if i % 2 == 0:
            w_in = ssm_w_in[j]
            wz = w_in[:, :D_INNER].astype(BF16)
            wx = w_in[:, D_INNER:D_INNER + CONV_DIM].astype(BF16)
            wdt = w_in[:, D_INNER + CONV_DIM:].reshape(D_MODEL, 2, SSM_GROUPS, HEADS_PER_GROUP)
            wdt = wdt.transpose(0, 2, 1, 3).reshape(D_MODEL, 2 * SSM_HEADS)
            wdt = jnp.pad(wdt, ((0, 0), (0, LANES - 2 * SSM_HEADS))).astype(BF16)
            z, xbc, dt_raw = _ssd_in(x, g_mix, wz, wx, wdt)
            dt_rows = dt_raw[:, :2 * SSM_HEADS].reshape(b, T, SSM_GROUPS, 2 * HEADS_PER_GROUP)
            dt_rows = dt_rows.transpose(0, 2, 3, 1)
            y = _ssd(xbc.reshape(b, T, CONV_DIM), z.reshape(b, T, D_INNER), dt_rows,
                     ssm_conv_w[j], ssm_conv_b[j].reshape(1, CONV_DIM),
                     _group_order(ssm_dt_bias[j])[:, :, None],
                     _group_order(ssm_A_log[j])[:, :, None],
                     jnp.repeat(ssm_D[j], SSM_HEAD_DIM).reshape(SSM_GROUPS, 1, GROUP_DIM),
                     ssm_norm_g[j].reshape(SSM_GROUPS, 1, GROUP_DIM))
            x = _proj_res(x, y.reshape(n, D_INNER), ssm_w_out[j].astype(BF16))
        else:
            q, k, vt = _attn_in(x.reshape(b, T, D_MODEL), g_mix, attn_w_qkv[j].astype(BF16),
                                jnp.tile(attn_q_norm[j], LANES // ATT_HEAD_DIM).reshape(1, LANES),
                                jnp.tile(attn_k_norm[j], LANES // ATT_HEAD_DIM).reshape(1, LANES),
                                cos, sin, head_ones)
            o = _flash(q, k, vt)
            x = _proj_res(x, o.reshape(n, N_HEADS * ATT_HEAD_DIM), attn_w_out[j].astype(BF16))
        x = _ffn(x, norm_g[i, 2].reshape(1, D_MODEL), wg[i, 1], wu[i, 1], wd[i, 1], fg,
                 i == depth - 1)
    x = x.reshape(b, T, D_MODEL)
    return (x[:nb_prompt], x[nb_prompt:])
```

```python
import functools
import math

import jax
import jax.numpy as jnp
from jax import lax
from jax.experimental import pallas as pl
from jax.experimental.pallas import tpu as pltpu

F32 = jnp.float32
BF16 = jnp.bfloat16

D_MODEL = 1024
GRID_W = 64
D_FF = 2816
D_INNER = 2048
SSM_HEAD_DIM = 64
SSM_HEADS = 32
SSM_GROUPS = 8
HEADS_PER_GROUP = 4
GROUP_DIM = HEADS_PER_GROUP * SSM_HEAD_DIM
D_STATE = 128
D_CONV = 5
CONV_DIM = D_INNER + 2 * SSM_GROUPS * D_STATE
ATT_HEAD_DIM = 64
N_HEADS = 16
N_KV_HEADS = 4
KV_REP = 4
AXIS_DIM = 32
ROPE_THETA = 10000.0
EPS = 1e-6

LANES = 128
SUBLANES = 8
VMEM_LIMIT = 56 * 1024 * 1024

TOKEN_TILE = 512
SSD_CHUNK = 128
CONV_ROWS = 256
CONV_HALO = 8
ATT_Q_TILE = 256
ATT_K_TILE = 512
FF_SPLITS = (1024, 1024, 768)

_RESIDENT = dict(pipeline_mode=pl.Buffered(1))


def _params(n_axes):
    return pltpu.CompilerParams(dimension_semantics=("parallel",) * n_axes,
                                vmem_limit_bytes=VMEM_LIMIT)


def _sigmoid(x):
    return 1.0 / (1.0 + jnp.exp(-x))


def _rms(x, g):
    return x * lax.rsqrt(jnp.mean(x * x, axis=-1, keepdims=True) + EPS) * g


def _ffn_kernel(x_ref, g_ref, wg_ref, wu_ref, wd_ref, fg_ref, o_ref, *, final):
    x = x_ref[...]
    h = _rms(x, g_ref[...]).astype(BF16)
    acc = None
    c0 = 0
    for cw in FF_SPLITS:
        gate = jnp.dot(h, wg_ref[:, c0:c0 + cw], preferred_element_type=F32)
        up = jnp.dot(h, wu_ref[:, c0:c0 + cw], preferred_element_type=F32)
        a = (gate * _sigmoid(gate) * up).astype(BF16)
        part = jnp.dot(a, wd_ref[c0:c0 + cw, :], preferred_element_type=F32)
        acc = part if acc is None else acc + part
        c0 += cw
    y = x + 0.5 * acc
    if final:
        y = _rms(y, fg_ref[...])
    o_ref[...] = y


def _ffn(x, g, wg, wu, wd, fg, final):
    n = x.shape[0]
    tm = min(TOKEN_TILE, n)
    row = lambda i: (i, 0)
    fixed = lambda i: (0, 0)
    return pl.pallas_call(
        functools.partial(_ffn_kernel, final=final),
        grid=(n // tm,),
        in_specs=[pl.BlockSpec((tm, D_MODEL), row),
                  pl.BlockSpec((1, D_MODEL), fixed, **_RESIDENT),
                  pl.BlockSpec((D_MODEL, D_FF), fixed, **_RESIDENT),
                  pl.BlockSpec((D_MODEL, D_FF), fixed, **_RESIDENT),
                  pl.BlockSpec((D_FF, D_MODEL), fixed, **_RESIDENT),
                  pl.BlockSpec((1, D_MODEL), fixed, **_RESIDENT)],
        out_specs=pl.BlockSpec((tm, D_MODEL), row),
        out_shape=jax.ShapeDtypeStruct((n, D_MODEL), F32),
        compiler_params=_params(1),
        name="ffn",
    )(x, g, wg, wu, wd, fg)


def _proj_res_kernel(x_ref, y_ref, w_ref, o_ref):
    o_ref[...] = x_ref[...] + jnp.dot(y_ref[...], w_ref[...], preferred_element_type=F32)


def _proj_res(x, y, w):
    n, k = y.shape
    tm = min(TOKEN_TILE, n)
    row = lambda i: (i, 0)
    return pl.pallas_call(
        _proj_res_kernel,
        grid=(n // tm,),
        in_specs=[pl.BlockSpec((tm, D_MODEL), row),
                  pl.BlockSpec((tm, k), row),
                  pl.BlockSpec((k, D_MODEL), lambda i: (0, 0), **_RESIDENT)],
        out_specs=pl.BlockSpec((tm, D_MODEL), row),
        out_shape=jax.ShapeDtypeStruct((n, D_MODEL), F32),
        compiler_params=_params(1),
        name="proj_res",
    )(x, y, w)


def _ssd_in_kernel(x_ref, g_ref, wz_ref, wx_ref, wdt_ref, z_ref, xbc_ref, dt_ref):
    h = _rms(x_ref[...], g_ref[...]).astype(BF16)
    z_ref[...] = jnp.dot(h, wz_ref[...], preferred_element_type=F32).astype(BF16)
    xbc_ref[...] = jnp.dot(h, wx_ref[...], preferred_element_type=F32).astype(BF16)
    dt_ref[...] = jnp.dot(h, wdt_ref[...], preferred_element_type=F32)


def _ssd_in(x, g, wz, wx, wdt):
    n = x.shape[0]
    tm = min(TOKEN_TILE, n)
    row = lambda i: (i, 0)
    fixed = lambda i: (0, 0)
    return pl.pallas_call(
        _ssd_in_kernel,
        grid=(n // tm,),
        in_specs=[pl.BlockSpec((tm, D_MODEL), row),
                  pl.BlockSpec((1, D_MODEL), fixed, **_RESIDENT),
                  pl.BlockSpec((D_MODEL, D_INNER), fixed, **_RESIDENT),
                  pl.BlockSpec((D_MODEL, CONV_DIM), fixed, **_RESIDENT),
                  pl.BlockSpec((D_MODEL, LANES), fixed, **_RESIDENT)],
        out_specs=[pl.BlockSpec((tm, D_INNER), row),
                   pl.BlockSpec((tm, CONV_DIM), row),
                   pl.BlockSpec((tm, LANES), row)],
        out_shape=[jax.ShapeDtypeStruct((n, D_INNER), BF16),
                   jax.ShapeDtypeStruct((n, CONV_DIM), BF16),
                   jax.ShapeDtypeStruct((n, LANES), F32)],
        compiler_params=_params(1),
        name="ssd_in",
    )(x, g, wz, wx, wdt)


_R_CS, _R_DT, _R_ECS, _R_WEND, _R_ETOT, _R_SPLIT = 0, 8, 16, 24, 32, 40
_N_ROWQ = 64
_NEG = -1e30


def _ssd_kernel(x_ref, b_ref, c_ref, z_ref, dt_ref,
                cwx_ref, cwb_ref, cwc_ref, cbx_ref, cbb_ref, cbc_ref,
                dtb_ref, alog_ref, d_ref, ng_ref,
                o_ref,
                pad_ref, xt_ref, bn_ref, ct_ref, rowq_ref, cscol_ref, sb_ref, fr_ref, ysb_ref):
    T = x_ref.shape[1]
    Q = SSD_CHUNK
    nc = T // Q
    R = min(CONV_ROWS, T)
    H = CONV_HALO
    hp = HEADS_PER_GROUP
    P = SSM_HEAD_DIM

    def conv_slab(src_ref, c0, w_ref, bias_ref, store):
        zeros = jnp.zeros((H, LANES), F32)
        pad_ref[0:H, :] = zeros
        pad_ref[T + H:T + 2 * H, :] = zeros

        def fill(i, carry):
            r0 = pl.multiple_of(i * R, R)
            pad_ref[pl.ds(r0 + H, R), :] = src_ref[0, pl.ds(r0, R), c0:c0 + LANES].astype(F32)
            return carry

        lax.fori_loop(0, T // R, fill, 0)
        w = w_ref[:, c0:c0 + LANES]
        bias = bias_ref[:, c0:c0 + LANES]

        def blk(i, carry):
            r0 = pl.multiple_of(i * R, R)
            v = pad_ref[pl.ds(r0, R + 2 * H), :]
            acc = jnp.broadcast_to(bias, (R + 2 * H, LANES))
            for k in range(D_CONV):
                sh = (D_CONV // 2 - k) % (R + 2 * H)
                vs = v if sh == 0 else pltpu.roll(v, sh, 0)
                acc = acc + w[k:k + 1, :] * vs
            y = acc[H:R + H]
            store(r0, y * _sigmoid(y))
            return carry

        lax.fori_loop(0, T // R, blk, 0, unroll=2)

    for s in range(GROUP_DIM // LANES):
        def store_x(r0, y, s=s):
            xt_ref[s * LANES:(s + 1) * LANES, pl.ds(r0, R)] = y.T
        conv_slab(x_ref, s * LANES, cwx_ref, cbx_ref, store_x)

    def store_b(r0, y):
        bn_ref[pl.ds(r0, R), :] = y.astype(BF16)
    conv_slab(b_ref, 0, cwb_ref, cbb_ref, store_b)

    def store_c(r0, y):
        ct_ref[:, pl.ds(r0, R)] = y.T.astype(BF16)
    conv_slab(c_ref, 0, cwc_ref, cbc_ref, store_c)

    raw = dt_ref[0, 0] + dtb_ref[0]
    dt = jnp.maximum(raw, 0.0) + jnp.log1p(jnp.exp(-jnp.abs(raw)))
    a = dt * (-jnp.exp(alog_ref[0]))
    lane = lax.broadcasted_iota(jnp.int32, (2 * hp, T), 1) % Q
    is_fwd = lax.broadcasted_iota(jnp.int32, (2 * hp, T), 0) < hp
    pre, suf = a, a
    s = 1
    while s < Q:
        pre = pre + jnp.where(lane >= s, pltpu.roll(pre, s, 1), 0.0)
        suf = suf + jnp.where(lane < Q - s, pltpu.roll(suf, T - s, 1), 0.0)
        s *= 2
    cs = jnp.where(is_fwd, pre, suf)
    rowq_ref[_R_CS:_R_CS + 8, :] = cs
    rowq_ref[_R_DT:_R_DT + 8, :] = dt
    rowq_ref[_R_ECS:_R_ECS + 8, :] = jnp.exp(cs)
    rowq_ref[_R_WEND:_R_WEND + 8, :] = jnp.exp(jnp.where(is_fwd, suf, pre) - a) * dt
    rowq_ref[_R_ETOT:_R_ETOT + 8, :] = jnp.exp(pre + suf - a)
    hi = cs.astype(BF16).astype(F32)
    mid = (cs - hi).astype(BF16).astype(F32)
    rowq_ref[_R_SPLIT:_R_SPLIT + 8, :] = hi
    rowq_ref[_R_SPLIT + 8:_R_SPLIT + 16, :] = mid
    rowq_ref[_R_SPLIT + 16:_R_SPLIT + 24, :] = cs - hi - mid

    def to_cols(c, carry):
        t0 = pl.multiple_of(c * Q, Q)
        tile = jnp.concatenate([rowq_ref[_R_SPLIT:_R_SPLIT + 24, pl.ds(t0, Q)],
                                jnp.zeros((LANES - 24, Q), F32)], axis=0)
        cscol_ref[pl.ds(t0, Q), :] = tile.T.astype(BF16)
        return carry

    lax.fori_loop(0, nc, to_cols, 0, unroll=8)

    def by_head(mat, rows, off):
        return jnp.concatenate([mat[r * P:(r + 1) * P] * rows[off + r:off + r + 1, :]
                                for r in range(hp)], axis=0)

    def state_step(S, t0, direction):
        off = direction * hp
        xw = by_head(xt_ref[:, pl.ds(t0, Q)], rowq_ref[_R_WEND:_R_WEND + 8, pl.ds(t0, Q)], off)
        upd = jnp.dot(xw.astype(BF16), bn_ref[pl.ds(t0, Q), :], preferred_element_type=F32)
        return by_head(S, rowq_ref[_R_ETOT:_R_ETOT + 8, pl.ds(t0, Q)], off) + upd

    def bwd_body(k, S):
        c = nc - 1 - k
        sb_ref[c] = S.astype(BF16)
        return state_step(S, pl.multiple_of(c * Q, Q), 1)

    lax.fori_loop(0, nc, bwd_body, jnp.zeros((GROUP_DIM, D_STATE), F32), unroll=4)

    src = lax.broadcasted_iota(jnp.int32, (Q, Q), 0)
    dst = lax.broadcasted_iota(jnp.int32, (Q, Q), 1)
    causal = dst >= src
    anti = dst <= src
    dmat = jnp.broadcast_to(d_ref[0], (GROUP_DIM, Q))
    sel_k = lax.broadcasted_iota(jnp.int32, (LANES, 2 * hp * Q), 0)
    sel_q = lax.broadcasted_iota(jnp.int32, (LANES, 2 * hp * Q), 1) // Q
    sel = jnp.where((sel_k < 24) & (sel_k % 8 == sel_q), 1.0, 0.0).astype(BF16)
    ones_g = jnp.ones((GROUP_DIM, LANES), BF16)
    ng = ng_ref[0]

    def front(c, slot):
        t0 = pl.multiple_of(c * Q, Q)
        ct = ct_ref[:, pl.ds(t0, Q)]
        fr_ref[slot, :, 0:Q] = jnp.dot(bn_ref[pl.ds(t0, Q), :], ct,
                                       preferred_element_type=F32)
        fr_ref[slot, :, Q:] = jnp.dot(cscol_ref[pl.ds(t0, Q), :], sel,
                                      preferred_element_type=F32)
        ysb_ref[slot] = jnp.dot(sb_ref[c], ct, preferred_element_type=F32)

    def back(c, slot, S):
        t0 = pl.multiple_of(c * Q, Q)
        ct = ct_ref[:, pl.ds(t0, Q)]
        xt = xt_ref[:, pl.ds(t0, Q)]
        csr = rowq_ref[_R_CS:_R_CS + 8, pl.ds(t0, Q)]
        dtr = rowq_ref[_R_DT:_R_DT + 8, pl.ds(t0, Q)]
        ecs = rowq_ref[_R_ECS:_R_ECS + 8, pl.ds(t0, Q)]
        cbt = fr_ref[slot, :, 0:Q]
        y_sf = jnp.dot(S.astype(BF16), ct, preferred_element_type=F32)
        ys = []
        for r in range(hp):
            cf = fr_ref[slot, :, (1 + r) * Q:(2 + r) * Q]
            cb = fr_ref[slot, :, (1 + hp + r) * Q:(2 + hp + r) * Q]
            mf = cbt * jnp.exp(jnp.where(causal, csr[r:r + 1, :] - cf, _NEG))
            mb = cbt * jnp.exp(jnp.where(anti, csr[hp + r:hp + r + 1, :] - cb, _NEG))
            xr = xt[r * P:(r + 1) * P]
            lhs = jnp.concatenate([xr * dtr[r:r + 1, :], xr * dtr[hp + r:hp + r + 1, :]], axis=1)
            rhs = jnp.concatenate([mf, mb], axis=0)
            ys.append(jnp.dot(lhs.astype(BF16), rhs.astype(BF16), preferred_element_type=F32))
        yt = jnp.concatenate(ys, axis=0)
        yt = yt + by_head(y_sf, ecs, 0) + by_head(ysb_ref[slot], ecs, hp) + dmat * xt
        y = yt.T
        zc = z_ref[0, pl.ds(t0, Q), :].astype(F32)
        y = y * (zc * _sigmoid(zc))
        sq = y * y
        sq_hi = sq.astype(BF16)
        sq_lo = (sq - sq_hi.astype(F32)).astype(BF16)
        ss = (jnp.dot(sq_hi, ones_g, preferred_element_type=F32)
              + jnp.dot(sq_lo, ones_g, preferred_element_type=F32))
        inv = lax.rsqrt(ss * (1.0 / GROUP_DIM) + EPS)
        yn = jnp.concatenate([y[:, s * LANES:(s + 1) * LANES] * inv
                              for s in range(GROUP_DIM // LANES)], axis=1) * ng
        o_ref[0, pl.ds(t0, Q), :] = yn.astype(BF16)
        return state_step(S, t0, 0)

    def pair(i, S):
        c = 2 * i
        front(c + 1, 1)
        S = back(c, 0, S)
        front(jnp.minimum(c + 2, nc - 1), 0)
        return back(c + 1, 1, S)

    front(0, 0)
    lax.fori_loop(0, nc // 2, pair, jnp.zeros((GROUP_DIM, D_STATE), F32))


def _ssd(xbc, z, dt_rows, conv_w, conv_b, dt_bias, a_log, d_vec, norm_g):
    b, T, _ = xbc.shape
    nb = D_INNER // D_STATE
    xblk = lambda i, g: (i, 0, g)
    bblk = lambda i, g: (i, 0, nb + g)
    cblk = lambda i, g: (i, 0, nb + SSM_GROUPS + g)
    per_g = lambda i, g: (g, 0, 0)
    return pl.pallas_call(
        _ssd_kernel,
        grid=(b, SSM_GROUPS),
        in_specs=[pl.BlockSpec((1, T, GROUP_DIM), xblk),
                  pl.BlockSpec((1, T, D_STATE), bblk),
                  pl.BlockSpec((1, T, D_STATE), cblk),
                  pl.BlockSpec((1, T, GROUP_DIM), xblk),
                  pl.BlockSpec((1, 1, 2 * HEADS_PER_GROUP, T), lambda i, g: (i, g, 0, 0)),
                  pl.BlockSpec((D_CONV, GROUP_DIM), lambda i, g: (0, g)),
                  pl.BlockSpec((D_CONV, D_STATE), lambda i, g: (0, nb + g)),
                  pl.BlockSpec((D_CONV, D_STATE), lambda i, g: (0, nb + SSM_GROUPS + g)),
                  pl.BlockSpec((1, GROUP_DIM), lambda i, g: (0, g)),
                  pl.BlockSpec((1, D_STATE), lambda i, g: (0, nb + g)),
                  pl.BlockSpec((1, D_STATE), lambda i, g: (0, nb + SSM_GROUPS + g)),
                  pl.BlockSpec((1, 2 * HEADS_PER_GROUP, 1), per_g),
                  pl.BlockSpec((1, 2 * HEADS_PER_GROUP, 1), per_g),
                  pl.BlockSpec((1, GROUP_DIM, 1), per_g),
                  pl.BlockSpec((1, 1, GROUP_DIM), per_g)],
        out_specs=pl.BlockSpec((1, T, GROUP_DIM), xblk),
        out_shape=jax.ShapeDtypeStruct((b, T, D_INNER), BF16),
        scratch_shapes=[pltpu.VMEM((T + 2 * CONV_HALO, LANES), F32),
                        pltpu.VMEM((GROUP_DIM, T), F32),
                        pltpu.VMEM((T, D_STATE), BF16),
                        pltpu.VMEM((D_STATE, T), BF16),
                        pltpu.VMEM((_N_ROWQ, T), F32),
                        pltpu.VMEM((T, LANES), BF16),
                        pltpu.VMEM((T // SSD_CHUNK, GROUP_DIM, D_STATE), BF16),
                        pltpu.VMEM((2, SSD_CHUNK, (1 + 2 * HEADS_PER_GROUP) * SSD_CHUNK), F32),
                        pltpu.VMEM((2, GROUP_DIM, SSD_CHUNK), F32)],
        compiler_params=_params(2),
        name="ssd",
    )(xbc, xbc, xbc, z, dt_rows, conv_w, conv_w, conv_w, conv_b, conv_b, conv_b,
      dt_bias, a_log, d_vec, norm_g)


def _attn_in_kernel(x_ref, g_ref, w_ref, qg_ref, kg_ref, cos_ref, sin_ref, ones_ref,
                    q_ref, k_ref, vt_ref):
    h = _rms(x_ref[0], g_ref[...]).astype(BF16)
    qkv = jnp.dot(h, w_ref[...], preferred_element_type=F32)
    cos = cos_ref[...]
    sin = sin_ref[...]
    ones = ones_ref[...]
    tm = h.shape[0]
    first_half = (lax.broadcasted_iota(jnp.int32, (tm, LANES), 1) % AXIS_DIM) < AXIS_DIM // 2
    qd = N_HEADS * ATT_HEAD_DIM
    kd = N_KV_HEADS * ATT_HEAD_DIM

    def norm_rope(v, gain):
        sq = v * v
        hi = sq.astype(BF16)
        lo = (sq - hi.astype(F32)).astype(BF16)
        ss = (jnp.dot(hi, ones, preferred_element_type=F32)
              + jnp.dot(lo, ones, preferred_element_type=F32))
        n = v * lax.rsqrt(ss * (1.0 / ATT_HEAD_DIM) + EPS) * gain
        partner = jnp.where(first_half,
                            pltpu.roll(n, LANES - AXIS_DIM // 2, 1),
                            pltpu.roll(n, AXIS_DIM // 2, 1))
        return n * cos + partner * sin

    qgain = qg_ref[...] * (ATT_HEAD_DIM ** -0.5 * math.log2(math.e))
    for s in range(qd // LANES):
        q_ref[0, :, s * LANES:(s + 1) * LANES] = norm_rope(
            qkv[:, s * LANES:(s + 1) * LANES], qgain).astype(BF16)
    for s in range(kd // LANES):
        kk = norm_rope(qkv[:, qd + s * LANES:qd + (s + 1) * LANES], kg_ref[...]).astype(BF16)
        k_ref[0, 2 * s] = kk[:, :ATT_HEAD_DIM]
        k_ref[0, 2 * s + 1] = kk[:, ATT_HEAD_DIM:]
    vt_ref[0] = qkv[:, qd + kd:].T.astype(BF16)


def _attn_in(x, g, w, qg, kg, cos, sin, ones):
    b, T, _ = x.shape
    tm = min(TOKEN_TILE, T)
    qkv_dim = (N_HEADS + 2 * N_KV_HEADS) * ATT_HEAD_DIM
    fixed = lambda i, t: (0, 0)
    return pl.pallas_call(
        _attn_in_kernel,
        grid=(b, T // tm),
        in_specs=[pl.BlockSpec((1, tm, D_MODEL), lambda i, t: (i, t, 0)),
                  pl.BlockSpec((1, D_MODEL), fixed, **_RESIDENT),
                  pl.BlockSpec((D_MODEL, qkv_dim), fixed, **_RESIDENT),
                  pl.BlockSpec((1, LANES), fixed, **_RESIDENT),
                  pl.BlockSpec((1, LANES), fixed, **_RESIDENT),
                  pl.BlockSpec((tm, LANES), lambda i, t: (t, 0)),
                  pl.BlockSpec((tm, LANES), lambda i, t: (t, 0)),
                  pl.BlockSpec((LANES, LANES), fixed, **_RESIDENT)],
        out_specs=[pl.BlockSpec((1, tm, N_HEADS * ATT_HEAD_DIM), lambda i, t: (i, t, 0)),
                   pl.BlockSpec((1, N_KV_HEADS, tm, ATT_HEAD_DIM), lambda i, t: (i, 0, t, 0)),
                   pl.BlockSpec((1, N_KV_HEADS * ATT_HEAD_DIM, tm), lambda i, t: (i, 0, t))],
        out_shape=[jax.ShapeDtypeStruct((b, T, N_HEADS * ATT_HEAD_DIM), BF16),
                   jax.ShapeDtypeStruct((b, N_KV_HEADS, T, ATT_HEAD_DIM), BF16),
                   jax.ShapeDtypeStruct((b, N_KV_HEADS * ATT_HEAD_DIM, T), BF16)],
        compiler_params=_params(2),
        name="attn_in",
    )(x, g, w, qg, kg, cos, sin, ones)


def _flash_kernel(q_ref, k_ref, vt_ref, o_ref, s_ref):
    tq = q_ref.shape[1]
    T = k_ref.shape[2]
    tk = s_ref.shape[1]
    dh = ATT_HEAD_DIM
    ones = jnp.ones((2 * SUBLANES, tk), BF16)
    nq = KV_REP * tq
    nk = T // tk

    def scores(g, c, slot):
        q4 = jnp.concatenate([q_ref[0, :, (g * KV_REP + r) * dh:(g * KV_REP + r + 1) * dh]
                              for r in range(KV_REP)], axis=0)
        k0 = pl.multiple_of(c * tk, tk)
        kc = k_ref[0, g, pl.ds(k0, tk), :]
        s_ref[slot] = lax.dot_general(kc, q4, (((1,), (1,)), ((), ())),
                                      preferred_element_type=F32)

    def attend(g, c, slot, m, acc):
        k0 = pl.multiple_of(c * tk, tk)
        va = jnp.concatenate([vt_ref[0, g * dh:(g + 1) * dh, pl.ds(k0, tk)], ones], axis=0)
        s = s_ref[slot]
        m_new = jnp.maximum(m, jnp.max(s, axis=0, keepdims=True))
        p = jnp.exp2(s - m_new).astype(BF16)
        acc = jnp.exp2(m - m_new) * acc + jnp.dot(va, p, preferred_element_type=F32)
        return m_new, acc

    scores(0, 0, 0)
    for g in range(N_KV_HEADS):

        def body(i, carry, g=g):
            m, acc = carry
            c = 2 * i
            scores(g, c + 1, 1)
            m, acc = attend(g, c, 0, m, acc)
            scores(g, c + 2, 0)
            return attend(g, c + 1, 1, m, acc)

        init = (jnp.full((1, nq), -jnp.inf, F32), jnp.zeros((dh + 2 * SUBLANES, nq), F32))
        m, acc = lax.fori_loop(0, nk // 2 - 1, body, init)
        scores(g, nk - 1, 1)
        m, acc = attend(g, nk - 2, 0, m, acc)
        if g + 1 < N_KV_HEADS:
            scores(g + 1, 0, 0)
        _, acc = attend(g, nk - 1, 1, m, acc)
        o4 = acc[:dh] / acc[dh:dh + 1]
        ot = jnp.concatenate([o4[:, r * tq:(r + 1) * tq] for r in range(KV_REP)], axis=0)
        o_ref[0, :, g * KV_REP * dh:(g + 1) * KV_REP * dh] = ot.T.astype(BF16)


def _flash(q, k, vt):
    b, T, qd = q.shape
    tq = min(ATT_Q_TILE, T)
    tk = min(ATT_K_TILE, T // 2)
    return pl.pallas_call(
        _flash_kernel,
        grid=(b, T // tq),
        in_specs=[pl.BlockSpec((1, tq, qd), lambda i, t: (i, t, 0)),
                  pl.BlockSpec((1, N_KV_HEADS, T, ATT_HEAD_DIM), lambda i, t: (i, 0, 0, 0)),
                  pl.BlockSpec((1, N_KV_HEADS * ATT_HEAD_DIM, T), lambda i, t: (i, 0, 0))],
        out_specs=pl.BlockSpec((1, tq, qd), lambda i, t: (i, t, 0)),
        out_shape=jax.ShapeDtypeStruct((b, T, qd), BF16),
        scratch_shapes=[pltpu.VMEM((2, tk, KV_REP * tq), F32)],
        compiler_params=_params(2),
        name="flash",
    )(q, k, vt)


def _rope_tables(T):
    t = jnp.arange(T, dtype=jnp.int32)
    pos = jnp.stack([(t // GRID_W).astype(F32), (t % GRID_W).astype(F32)], axis=1)
    inv_freq = ROPE_THETA ** (-jnp.arange(0, AXIS_DIM, 2, dtype=F32) / AXIS_DIM)
    ang = pos[:, :, None] * inv_freq[None, None, :]
    cos = jnp.cos(ang)
    sin = jnp.sin(ang)
    cos_d = jnp.concatenate([cos, cos], axis=-1).reshape(T, ATT_HEAD_DIM)
    sin_d = jnp.concatenate([-sin, sin], axis=-1).reshape(T, ATT_HEAD_DIM)
    return jnp.tile(cos_d, (1, LANES // ATT_HEAD_DIM)), jnp.tile(sin_d, (1, LANES // ATT_HEAD_DIM))


def _group_order(v):
    return v.reshape(2, SSM_GROUPS, HEADS_PER_GROUP).transpose(1, 0, 2).reshape(SSM_GROUPS, 2 * HEADS_PER_GROUP)


def kernel(x_prompt, x_sample, norm_g, ffn_w_gate, ffn_w_up, ffn_w_down, ssm_w_in, ssm_conv_w,
           ssm_conv_b, ssm_dt_bias, ssm_A_log, ssm_D, ssm_norm_g, ssm_w_out, attn_w_qkv,
           attn_q_norm, attn_k_norm, attn_w_out, final_norm):
    nb_prompt = x_prompt.shape[0]
    x = jnp.concatenate([x_prompt, x_sample], axis=0)
    b, T, _ = x.shape
    n = b * T
    x = x.reshape(n, D_MODEL)
    depth = norm_g.shape[0]

    wg = ffn_w_gate.astype(BF16)
    wu = ffn_w_up.astype(BF16)
    wd = ffn_w_down.astype(BF16)
    fg = final_norm.reshape(1, D_MODEL)
    cos, sin = _rope_tables(T)
    half = jnp.arange(LANES) // ATT_HEAD_DIM
    head_ones = (half[:, None] == half[None, :]).astype(BF16)

    for i in range(depth):
        j = i // 2
        x = _ffn(x, norm_g[i, 0].reshape(1, D_MODEL), wg[i, 0], wu[i, 0], wd[i, 0], fg, False)
        g_mix = norm_g[i, 1].reshape(1, D_MODEL)
        if i % 2 == 0:
            w_in = ssm_w_in[j]
            wz = w_in[:, :D_INNER].astype(BF16)
            wx = w_in[:, D_INNER:D_INNER + CONV_DIM].astype(BF16)
            wdt = w_in[:, D_INNER + CONV_DIM:].reshape(D_MODEL, 2, SSM_GROUPS, HEADS_PER_GROUP)
            wdt = wdt.transpose(0, 2, 1, 3).reshape(D_MODEL, 2 * SSM_HEADS)
            wdt = jnp.pad(wdt, ((0, 0), (0, LANES - 2 * SSM_HEADS))).astype(BF16)
            z, xbc, dt_raw = _ssd_in(x, g_mix, wz, wx, wdt)
            dt_rows = dt_raw[:, :2 * SSM_HEADS].reshape(b, T, SSM_GROUPS, 2 * HEADS_PER_GROUP)
            dt_rows = dt_rows.transpose(0, 2, 3, 1)
            y = _ssd(xbc.reshape(b, T, CONV_DIM), z.reshape(b, T, D_INNER), dt_rows,
                     ssm_conv_w[j], ssm_conv_b[j].reshape(1, CONV_DIM),
                     _group_order(ssm_dt_bias[j])[:, :, None],
                     _group_order(ssm_A_log[j])[:, :, None],
                     jnp.repeat(ssm_D[j], SSM_HEAD_DIM).reshape(SSM_GROUPS, GROUP_DIM, 1),
                     ssm_norm_g[j].reshape(SSM_GROUPS, 1, GROUP_DIM))
            x = _proj_res(x, y.reshape(n, D_INNER), ssm_w_out[j].astype(BF16))
        else:
            q, k, vt = _attn_in(x.reshape(b, T, D_MODEL), g_mix, attn_w_qkv[j].astype(BF16),
                                jnp.tile(attn_q_norm[j], LANES // ATT_HEAD_DIM).reshape(1, LANES),
                                jnp.tile(attn_k_norm[j], LANES // ATT_HEAD_DIM).reshape(1, LANES),
                                cos, sin, head_ones)
            o = _flash(q, k, vt)
            x = _proj_res(x, o.reshape(n, N_HEADS * ATT_HEAD_DIM), attn_w_out[j].astype(BF16))
        x = _ffn(x, norm_g[i, 2].reshape(1, D_MODEL), wg[i, 1], wu[i, 1], wd[i, 1], fg,
                 i == depth - 1)
    x = x.reshape(b, T, D_MODEL)
    return (x[:nb_prompt], x[nb_prompt:])
```

```python
import functools
import math

import jax
import jax.numpy as jnp
from jax import lax
from jax.experimental import pallas as pl
from jax.experimental.pallas import tpu as pltpu

F32 = jnp.float32
BF16 = jnp.bfloat16

D_MODEL = 1024
GRID_W = 64
D_FF = 2816
D_INNER = 2048
SSM_HEAD_DIM = 64
SSM_HEADS = 32
SSM_GROUPS = 8
HEADS_PER_GROUP = 4
GROUP_DIM = HEADS_PER_GROUP * SSM_HEAD_DIM
D_STATE = 128
D_CONV = 5
CONV_DIM = D_INNER + 2 * SSM_GROUPS * D_STATE
ATT_HEAD_DIM = 64
N_HEADS = 16
N_KV_HEADS = 4
KV_REP = 4
AXIS_DIM = 32
ROPE_THETA = 10000.0
EPS = 1e-6

LANES = 128
SUBLANES = 8
VMEM_LIMIT = 56 * 1024 * 1024

TOKEN_TILE = 512
SSD_CHUNK = 128
SSD_TRIP = 4
CONV_ROWS = 256
CONV_HALO = 8
ATT_Q_TILE = 256
ATT_K_TILE = 512
ATT_TRIP = 4
FF_SPLITS = (1024, 1024, 768)

_RESIDENT = dict(pipeline_mode=pl.Buffered(1))


def _params(n_axes):
    return pltpu.CompilerParams(dimension_semantics=("parallel",) * n_axes,
                                vmem_limit_bytes=VMEM_LIMIT)


def _sigmoid(x):
    return 1.0 / (1.0 + jnp.exp(-x))


def _rms(x, g):
    return x * lax.rsqrt(jnp.mean(x * x, axis=-1, keepdims=True) + EPS) * g


def _ffn_kernel(*refs, final, mixed):
    if mixed:
        x_ref, y_ref, wo_ref, g_ref, wg_ref, wu_ref, wd_ref, fg_ref, o_ref = refs
        x = x_ref[...] + jnp.dot(y_ref[...], wo_ref[...], preferred_element_type=F32)
    else:
        x_ref, g_ref, wg_ref, wu_ref, wd_ref, fg_ref, o_ref = refs
        x = x_ref[...]
    h = _rms(x, g_ref[...]).astype(BF16)
    acc = None
    c0 = 0
    for cw in FF_SPLITS:
        gate = jnp.dot(h, wg_ref[:, c0:c0 + cw], preferred_element_type=F32)
        up = jnp.dot(h, wu_ref[:, c0:c0 + cw], preferred_element_type=F32)
        a = (gate * _sigmoid(gate) * up).astype(BF16)
        part = jnp.dot(a, wd_ref[c0:c0 + cw, :], preferred_element_type=F32)
        acc = part if acc is None else acc + part
        c0 += cw
    y = x + 0.5 * acc
    if final:
        y = _rms(y, fg_ref[...])
    o_ref[...] = y


def _ffn(x, g, wg, wu, wd, fg, final, mix=None):
    n = x.shape[0]
    tm = min(TOKEN_TILE, n)
    row = lambda i: (i, 0)
    fixed = lambda i: (0, 0)
    args = [x]
    in_specs = [pl.BlockSpec((tm, D_MODEL), row)]
    if mix is not None:
        y, wo = mix
        k = y.shape[1]
        args += [y, wo]
        in_specs += [pl.BlockSpec((tm, k), row), pl.BlockSpec((k, D_MODEL), fixed, **_RESIDENT)]
    args += [g, wg, wu, wd, fg]
    in_specs += [pl.BlockSpec((1, D_MODEL), fixed, **_RESIDENT),
                 pl.BlockSpec((D_MODEL, D_FF), fixed, **_RESIDENT),
                 pl.BlockSpec((D_MODEL, D_FF), fixed, **_RESIDENT),
                 pl.BlockSpec((D_FF, D_MODEL), fixed, **_RESIDENT),
                 pl.BlockSpec((1, D_MODEL), fixed, **_RESIDENT)]
    return pl.pallas_call(
        functools.partial(_ffn_kernel, final=final, mixed=mix is not None),
        grid=(n // tm,),
        in_specs=in_specs,
        out_specs=pl.BlockSpec((tm, D_MODEL), row),
        out_shape=jax.ShapeDtypeStruct((n, D_MODEL), F32),
        compiler_params=_params(1),
        name="ffn",
    )(*args)


def _ssd_in_kernel(x_ref, g_ref, wz_ref, wx_ref, wdt_ref, z_ref, xbc_ref, dt_ref):
    h = _rms(x_ref[...], g_ref[...]).astype(BF16)
    z_ref[...] = jnp.dot(h, wz_ref[...], preferred_element_type=F32).astype(BF16)
    xbc_ref[...] = jnp.dot(h, wx_ref[...], preferred_element_type=F32).astype(BF16)
    dt_ref[...] = jnp.dot(h, wdt_ref[...], preferred_element_type=F32)


def _ssd_in(x, g, wz, wx, wdt):
    n = x.shape[0]
    tm = min(TOKEN_TILE, n)
    row = lambda i: (i, 0)
    fixed = lambda i: (0, 0)
    return pl.pallas_call(
        _ssd_in_kernel,
        grid=(n // tm,),
        in_specs=[pl.BlockSpec((tm, D_MODEL), row),
                  pl.BlockSpec((1, D_MODEL), fixed, **_RESIDENT),
                  pl.BlockSpec((D_MODEL, D_INNER), fixed, **_RESIDENT),
                  pl.BlockSpec((D_MODEL, CONV_DIM), fixed, **_RESIDENT),
                  pl.BlockSpec((D_MODEL, LANES), fixed, **_RESIDENT)],
        out_specs=[pl.BlockSpec((tm, D_INNER), row),
                   pl.BlockSpec((tm, CONV_DIM), row),
                   pl.BlockSpec((tm, LANES), row)],
        out_shape=[jax.ShapeDtypeStruct((n, D_INNER), BF16),
                   jax.ShapeDtypeStruct((n, CONV_DIM), BF16),
                   jax.ShapeDtypeStruct((n, LANES), F32)],
        compiler_params=_params(1),
        name="ssd_in",
    )(x, g, wz, wx, wdt)


_R_CS, _R_DT, _R_ECS, _R_WEND, _R_ETOT, _R_SPLIT = 0, 8, 16, 24, 32, 40
_N_ROWQ = 64
_NEG = -1e30


def _ssd_kernel(x_ref, b_ref, c_ref, z_ref, dt_ref,
                cwx_ref, cwb_ref, cwc_ref, cbx_ref, cbb_ref, cbc_ref,
                dtb_ref, alog_ref, d_ref, ng_ref,
                o_ref,
                pad_ref, xt_ref, bn_ref, ct_ref, rowq_ref, cscol_ref, sb_ref, fr_ref, ysb_ref, upd_ref):
    T = x_ref.shape[1]
    Q = SSD_CHUNK
    nc = T // Q
    R = min(CONV_ROWS, T)
    H = CONV_HALO
    hp = HEADS_PER_GROUP
    P = SSM_HEAD_DIM

    def conv_slab(src_ref, c0, w_ref, bias_ref, store):
        zeros = jnp.zeros((H, LANES), F32)
        pad_ref[0:H, :] = zeros
        pad_ref[T + H:T + 2 * H, :] = zeros

        def fill(i, carry):
            r0 = pl.multiple_of(i * R, R)
            pad_ref[pl.ds(r0 + H, R), :] = src_ref[0, pl.ds(r0, R), c0:c0 + LANES].astype(F32)
            return carry

        lax.fori_loop(0, T // R, fill, 0)
        w = w_ref[:, c0:c0 + LANES]
        bias = bias_ref[:, c0:c0 + LANES]

        def blk(i, carry):
            r0 = pl.multiple_of(i * R, R)
            y = jnp.broadcast_to(bias, (R, LANES))
            for k in range(D_CONV):
                y = y + w[k:k + 1, :] * pad_ref[pl.ds(r0 + (H - D_CONV // 2 + k), R), :]
            store(r0, y * _sigmoid(y))
            return carry

        lax.fori_loop(0, T // R, blk, 0, unroll=4)

    for s in range(GROUP_DIM // LANES):
        def store_x(r0, y, s=s):
            xt_ref[s * LANES:(s + 1) * LANES, pl.ds(r0, R)] = y.T
        conv_slab(x_ref, s * LANES, cwx_ref, cbx_ref, store_x)

    def store_b(r0, y):
        bn_ref[pl.ds(r0, R), :] = y.astype(BF16)
    conv_slab(b_ref, 0, cwb_ref, cbb_ref, store_b)

    def store_c(r0, y):
        ct_ref[:, pl.ds(r0, R)] = y.T.astype(BF16)
    conv_slab(c_ref, 0, cwc_ref, cbc_ref, store_c)

    raw = dt_ref[0, 0] + dtb_ref[0]
    dt = jnp.maximum(raw, 0.0) + jnp.log1p(jnp.exp(-jnp.abs(raw)))
    a = dt * (-jnp.exp(alog_ref[0]) * math.log2(math.e))
    lane = lax.broadcasted_iota(jnp.int32, (2 * hp, T), 1) % Q
    is_fwd = lax.broadcasted_iota(jnp.int32, (2 * hp, T), 0) < hp
    pre, suf = a, a
    s = 1
    while s < Q:
        pre = pre + jnp.where(lane >= s, pltpu.roll(pre, s, 1), 0.0)
        suf = suf + jnp.where(lane < Q - s, pltpu.roll(suf, T - s, 1), 0.0)
        s *= 2
    cs = jnp.where(is_fwd, pre, suf)
    rowq_ref[_R_CS:_R_CS + 8, :] = cs
    rowq_ref[_R_DT:_R_DT + 8, :] = dt
    rowq_ref[_R_ECS:_R_ECS + 8, :] = jnp.exp2(cs)
    rowq_ref[_R_WEND:_R_WEND + 8, :] = jnp.exp2(jnp.where(is_fwd, suf, pre) - a) * dt
    rowq_ref[_R_ETOT:_R_ETOT + 8, :] = jnp.exp2(pre + suf - a)
    hi = cs.astype(BF16).astype(F32)
    mid = (cs - hi).astype(BF16).astype(F32)
    rowq_ref[_R_SPLIT:_R_SPLIT + 8, :] = hi
    rowq_ref[_R_SPLIT + 8:_R_SPLIT + 16, :] = mid
    rowq_ref[_R_SPLIT + 16:_R_SPLIT + 24, :] = cs - hi - mid

    def to_cols(c, carry):
        t0 = pl.multiple_of(c * Q, Q)
        tile = jnp.concatenate([rowq_ref[_R_SPLIT:_R_SPLIT + 24, pl.ds(t0, Q)],
                                jnp.zeros((LANES - 24, Q), F32)], axis=0)
        cscol_ref[pl.ds(t0, Q), :] = tile.T.astype(BF16)
        return carry

    lax.fori_loop(0, nc, to_cols, 0, unroll=8)

    def by_head(mat, rows, off):
        return jnp.concatenate([mat[r * P:(r + 1) * P] * rows[off + r:off + r + 1, :]
                                for r in range(hp)], axis=0)

    def state_update(t0, direction):
        xw = by_head(xt_ref[:, pl.ds(t0, Q)], rowq_ref[_R_WEND:_R_WEND + 8, pl.ds(t0, Q)],
                     direction * hp)
        return jnp.dot(xw.astype(BF16), bn_ref[pl.ds(t0, Q), :], preferred_element_type=F32)

    def state_decay(S, t0, direction):
        return by_head(S, rowq_ref[_R_ETOT:_R_ETOT + 8, pl.ds(t0, Q)], direction * hp)

    def bwd_body(k, S):
        c = nc - 1 - k
        sb_ref[c] = S.astype(BF16)
        t0 = pl.multiple_of(c * Q, Q)
        return state_decay(S, t0, 1) + state_update(t0, 1)

    lax.fori_loop(0, nc, bwd_body, jnp.zeros((GROUP_DIM, D_STATE), F32), unroll=4)

    src = lax.broadcasted_iota(jnp.int32, (Q, Q), 0)
    dst = lax.broadcasted_iota(jnp.int32, (Q, Q), 1)
    causal = dst >= src
    anti = dst <= src
    dmat = jnp.broadcast_to(d_ref[0], (GROUP_DIM, Q))
    sel_k = lax.broadcasted_iota(jnp.int32, (LANES, 2 * hp * Q), 0)
    sel_q = lax.broadcasted_iota(jnp.int32, (LANES, 2 * hp * Q), 1) // Q
    sel = jnp.where((sel_k < 24) & (sel_k % 8 == sel_q), 1.0, 0.0).astype(BF16)
    ng = ng_ref[0]

    def front(c, slot):
        t0 = pl.multiple_of(c * Q, Q)
        ct = ct_ref[:, pl.ds(t0, Q)]
        fr_ref[slot, :, 0:Q] = jnp.dot(bn_ref[pl.ds(t0, Q), :], ct,
                                       preferred_element_type=F32)
        fr_ref[slot, :, Q:] = jnp.dot(cscol_ref[pl.ds(t0, Q), :], sel,
                                      preferred_element_type=F32)
        ysb_ref[slot] = jnp.dot(sb_ref[c], ct, preferred_element_type=F32)
        upd_ref[slot] = state_update(t0, 0)

    def back(c, slot, S):
        t0 = pl.multiple_of(c * Q, Q)
        ct = ct_ref[:, pl.ds(t0, Q)]
        xt = xt_ref[:, pl.ds(t0, Q)]
        csr = rowq_ref[_R_CS:_R_CS + 8, pl.ds(t0, Q)]
        dtr = rowq_ref[_R_DT:_R_DT + 8, pl.ds(t0, Q)]
        ecs = rowq_ref[_R_ECS:_R_ECS + 8, pl.ds(t0, Q)]
        cbt = fr_ref[slot, :, 0:Q]
        y_sf = jnp.dot(S.astype(BF16), ct, preferred_element_type=F32)
        ys = []
        for r in range(hp):
            cf = fr_ref[slot, :, (1 + r) * Q:(2 + r) * Q]
            cb = fr_ref[slot, :, (1 + hp + r) * Q:(2 + hp + r) * Q]
            mf = cbt * jnp.exp2(jnp.where(causal, csr[r:r + 1, :] - cf, _NEG))
            mb = cbt * jnp.exp2(jnp.where(anti, csr[hp + r:hp + r + 1, :] - cb, _NEG))
            xr = xt[r * P:(r + 1) * P]
            lhs = jnp.concatenate([xr * dtr[r:r + 1, :], xr * dtr[hp + r:hp + r + 1, :]], axis=1)
            rhs = jnp.concatenate([mf, mb], axis=0)
            ys.append(jnp.dot(lhs.astype(BF16), rhs.astype(BF16), preferred_element_type=F32))
        yt = jnp.concatenate(ys, axis=0)
        yt = yt + by_head(y_sf, ecs, 0) + by_head(ysb_ref[slot], ecs, hp) + dmat * xt
        y = yt.T
        zc = z_ref[0, pl.ds(t0, Q), :].astype(F32)
        y = y * (zc * _sigmoid(zc))
        o_ref[0, pl.ds(t0, Q), :] = _rms(y, ng).astype(BF16)
        return state_decay(S, t0, 0) + upd_ref[slot]

    def trip(i, S):
        c = SSD_TRIP * i
        for u in range(SSD_TRIP):
            front(jnp.minimum(c + u + 1, nc - 1), (u + 1) % 2)
            S = back(c + u, u % 2, S)
        return S

    front(0, 0)
    lax.fori_loop(0, nc // SSD_TRIP, trip, jnp.zeros((GROUP_DIM, D_STATE), F32))


def _ssd(xbc, z, dt_rows, conv_w, conv_b, dt_bias, a_log, d_vec, norm_g):
    b, T, _ = xbc.shape
    nb = D_INNER // D_STATE
    xblk = lambda i, g: (i, 0, g)
    bblk = lambda i, g: (i, 0, nb + g)
    cblk = lambda i, g: (i, 0, nb + SSM_GROUPS + g)
    per_g = lambda i, g: (g, 0, 0)
    return pl.pallas_call(
        _ssd_kernel,
        grid=(b, SSM_GROUPS),
        in_specs=[pl.BlockSpec((1, T, GROUP_DIM), xblk),
                  pl.BlockSpec((1, T, D_STATE), bblk),
                  pl.BlockSpec((1, T, D_STATE), cblk),
                  pl.BlockSpec((1, T, GROUP_DIM), xblk),
                  pl.BlockSpec((1, 1, 2 * HEADS_PER_GROUP, T), lambda i, g: (i, g, 0, 0)),
                  pl.BlockSpec((D_CONV, GROUP_DIM), lambda i, g: (0, g)),
                  pl.BlockSpec((D_CONV, D_STATE), lambda i, g: (0, nb + g)),
                  pl.BlockSpec((D_CONV, D_STATE), lambda i, g: (0, nb + SSM_GROUPS + g)),
                  pl.BlockSpec((1, GROUP_DIM), lambda i, g: (0, g)),
                  pl.BlockSpec((1, D_STATE), lambda i, g: (0, nb + g)),
                  pl.BlockSpec((1, D_STATE), lambda i, g: (0, nb + SSM_GROUPS + g)),
                  pl.BlockSpec((1, 2 * HEADS_PER_GROUP, 1), per_g),
                  pl.BlockSpec((1, 2 * HEADS_PER_GROUP, 1), per_g),
                  pl.BlockSpec((1, GROUP_DIM, 1), per_g),
                  pl.BlockSpec((1, 1, GROUP_DIM), per_g)],
        out_specs=pl.BlockSpec((1, T, GROUP_DIM), xblk),
        out_shape=jax.ShapeDtypeStruct((b, T, D_INNER), BF16),
        scratch_shapes=[pltpu.VMEM((T + 2 * CONV_HALO, LANES), F32),
                        pltpu.VMEM((GROUP_DIM, T), F32),
                        pltpu.VMEM((T, D_STATE), BF16),
                        pltpu.VMEM((D_STATE, T), BF16),
                        pltpu.VMEM((_N_ROWQ, T), F32),
                        pltpu.VMEM((T, LANES), BF16),
                        pltpu.VMEM((T // SSD_CHUNK, GROUP_DIM, D_STATE), BF16),
                        pltpu.VMEM((2, SSD_CHUNK, (1 + 2 * HEADS_PER_GROUP) * SSD_CHUNK), F32),
                        pltpu.VMEM((2, GROUP_DIM, SSD_CHUNK), F32),
                        pltpu.VMEM((2, GROUP_DIM, D_STATE), F32)],
        compiler_params=_params(2),
        name="ssd",
    )(xbc, xbc, xbc, z, dt_rows, conv_w, conv_w, conv_w, conv_b, conv_b, conv_b,
      dt_bias, a_log, d_vec, norm_g)


def _attn_in_kernel(x_ref, g_ref, w_ref, qg_ref, kg_ref, cos_ref, sin_ref, ones_ref,
                    q_ref, k_ref, vt_ref):
    h = _rms(x_ref[0], g_ref[...]).astype(BF16)
    qkv = jnp.dot(h, w_ref[...], preferred_element_type=F32)
    cos = cos_ref[...]
    sin = sin_ref[...]
    ones = ones_ref[...]
    tm = h.shape[0]
    first_half = (lax.broadcasted_iota(jnp.int32, (tm, LANES), 1) % AXIS_DIM) < AXIS_DIM // 2
    qd = N_HEADS * ATT_HEAD_DIM
    kd = N_KV_HEADS * ATT_HEAD_DIM

    def norm_rope(v, gain):
        sq = v * v
        hi = sq.astype(BF16)
        lo = (sq - hi.astype(F32)).astype(BF16)
        ss = (jnp.dot(hi, ones, preferred_element_type=F32)
              + jnp.dot(lo, ones, preferred_element_type=F32))
        n = v * lax.rsqrt(ss * (1.0 / ATT_HEAD_DIM) + EPS) * gain
        partner = jnp.where(first_half,
                            pltpu.roll(n, LANES - AXIS_DIM // 2, 1),
                            pltpu.roll(n, AXIS_DIM // 2, 1))
        return n * cos + partner * sin

    qgain = qg_ref[...] * (ATT_HEAD_DIM ** -0.5 * math.log2(math.e))
    for s in range(qd // LANES):
        q_ref[0, :, s * LANES:(s + 1) * LANES] = norm_rope(
            qkv[:, s * LANES:(s + 1) * LANES], qgain).astype(BF16)
    for s in range(kd // LANES):
        kk = norm_rope(qkv[:, qd + s * LANES:qd + (s + 1) * LANES], kg_ref[...]).astype(BF16)
        k_ref[0, 2 * s] = kk[:, :ATT_HEAD_DIM]
        k_ref[0, 2 * s + 1] = kk[:, ATT_HEAD_DIM:]
    vt_ref[0] = qkv[:, qd + kd:].T.astype(BF16)


def _attn_in(x, g, w, qg, kg, cos, sin, ones):
    b, T, _ = x.shape
    tm = min(TOKEN_TILE, T)
    qkv_dim = (N_HEADS + 2 * N_KV_HEADS) * ATT_HEAD_DIM
    fixed = lambda i, t: (0, 0)
    return pl.pallas_call(
        _attn_in_kernel,
        grid=(b, T // tm),
        in_specs=[pl.BlockSpec((1, tm, D_MODEL), lambda i, t: (i, t, 0)),
                  pl.BlockSpec((1, D_MODEL), fixed, **_RESIDENT),
                  pl.BlockSpec((D_MODEL, qkv_dim), fixed, **_RESIDENT),
                  pl.BlockSpec((1, LANES), fixed, **_RESIDENT),
                  pl.BlockSpec((1, LANES), fixed, **_RESIDENT),
                  pl.BlockSpec((tm, LANES), lambda i, t: (t, 0)),
                  pl.BlockSpec((tm, LANES), lambda i, t: (t, 0)),
                  pl.BlockSpec((LANES, LANES), fixed, **_RESIDENT)],
        out_specs=[pl.BlockSpec((1, tm, N_HEADS * ATT_HEAD_DIM), lambda i, t: (i, t, 0)),
                   pl.BlockSpec((1, N_KV_HEADS, tm, ATT_HEAD_DIM), lambda i, t: (i, 0, t, 0)),
                   pl.BlockSpec((1, N_KV_HEADS * ATT_HEAD_DIM, tm), lambda i, t: (i, 0, t))],
        out_shape=[jax.ShapeDtypeStruct((b, T, N_HEADS * ATT_HEAD_DIM), BF16),
                   jax.ShapeDtypeStruct((b, N_KV_HEADS, T, ATT_HEAD_DIM), BF16),
                   jax.ShapeDtypeStruct((b, N_KV_HEADS * ATT_HEAD_DIM, T), BF16)],
        compiler_params=_params(2),
        name="attn_in",
    )(x, g, w, qg, kg, cos, sin, ones)


def _flash_kernel(q_ref, k_ref, vt_ref, o_ref, s_ref):
    tq = q_ref.shape[1]
    T = k_ref.shape[2]
    tk = s_ref.shape[1]
    dh = ATT_HEAD_DIM
    ones = jnp.ones((2 * SUBLANES, tk), BF16)
    nq = KV_REP * tq
    nk = T // tk

    def scores(g, c, slot):
        q4 = jnp.concatenate([q_ref[0, :, (g * KV_REP + r) * dh:(g * KV_REP + r + 1) * dh]
                              for r in range(KV_REP)], axis=0)
        k0 = pl.multiple_of(c * tk, tk)
        kc = k_ref[0, g, pl.ds(k0, tk), :]
        s_ref[slot] = lax.dot_general(kc, q4, (((1,), (1,)), ((), ())),
                                      preferred_element_type=F32)

    def attend(g, c, slot, m, acc):
        k0 = pl.multiple_of(c * tk, tk)
        va = jnp.concatenate([vt_ref[0, g * dh:(g + 1) * dh, pl.ds(k0, tk)], ones], axis=0)
        s = s_ref[slot]
        m_new = jnp.maximum(m, jnp.max(s, axis=0, keepdims=True))
        p = jnp.exp2(s - m_new).astype(BF16)
        acc = jnp.exp2(m - m_new) * acc + jnp.dot(va, p, preferred_element_type=F32)
        return m_new, acc

    scores(0, 0, 0)
    per_trip = min(ATT_TRIP, nk)
    for g in range(N_KV_HEADS):

        def body(i, carry, g=g):
            m, acc = carry
            c = per_trip * i
            for u in range(per_trip):
                scores(g, c + u + 1, (u + 1) % 2)
                m, acc = attend(g, c + u, u % 2, m, acc)
            return m, acc

        init = (jnp.full((1, nq), -jnp.inf, F32), jnp.zeros((dh + 2 * SUBLANES, nq), F32))
        m, acc = lax.fori_loop(0, nk // per_trip - 1, body, init)
        for u in range(per_trip):
            c = nk - per_trip + u
            if c + 1 < nk:
                scores(g, c + 1, (u + 1) % 2)
            elif g + 1 < N_KV_HEADS:
                scores(g + 1, 0, 0)
            m, acc = attend(g, c, u % 2, m, acc)
        o4 = acc[:dh] / acc[dh:dh + 1]
        ot = jnp.concatenate([o4[:, r * tq:(r + 1) * tq] for r in range(KV_REP)], axis=0)
        o_ref[0, :, g * KV_REP * dh:(g + 1) * KV_REP * dh] = ot.T.astype(BF16)


def _flash(q, k, vt):
    b, T, qd = q.shape
    tq = min(ATT_Q_TILE, T)
    tk = min(ATT_K_TILE, T // 2)
    return pl.pallas_call(
        _flash_kernel,
        grid=(b, T // tq),
        in_specs=[pl.BlockSpec((1, tq, qd), lambda i, t: (i, t, 0)),
                  pl.BlockSpec((1, N_KV_HEADS, T, ATT_HEAD_DIM), lambda i, t: (i, 0, 0, 0)),
                  pl.BlockSpec((1, N_KV_HEADS * ATT_HEAD_DIM, T), lambda i, t: (i, 0, 0))],
        out_specs=pl.BlockSpec((1, tq, qd), lambda i, t: (i, t, 0)),
        out_shape=jax.ShapeDtypeStruct((b, T, qd), BF16),
        scratch_shapes=[pltpu.VMEM((2, tk, KV_REP * tq), F32)],
        compiler_params=_params(2),
        name="flash",
    )(q, k, vt)


def _rope_tables(T):
    t = jnp.arange(T, dtype=jnp.int32)
    pos = jnp.stack([(t // GRID_W).astype(F32), (t % GRID_W).astype(F32)], axis=1)
    inv_freq = ROPE_THETA ** (-jnp.arange(0, AXIS_DIM, 2, dtype=F32) / AXIS_DIM)
    ang = pos[:, :, None] * inv_freq[None, None, :]
    cos = jnp.cos(ang)
    sin = jnp.sin(ang)
    cos_d = jnp.concatenate([cos, cos], axis=-1).reshape(T, ATT_HEAD_DIM)
    sin_d = jnp.concatenate([-sin, sin], axis=-1).reshape(T, ATT_HEAD_DIM)
    return jnp.tile(cos_d, (1, LANES // ATT_HEAD_DIM)), jnp.tile(sin_d, (1, LANES // ATT_HEAD_DIM))


def _group_order(v):
    return v.reshape(2, SSM_GROUPS, HEADS_PER_GROUP).transpose(1, 0, 2).reshape(SSM_GROUPS, 2 * HEADS_PER_GROUP)


def kernel(x_prompt, x_sample, norm_g, ffn_w_gate, ffn_w_up, ffn_w_down, ssm_w_in, ssm_conv_w,
           ssm_conv_b, ssm_dt_bias, ssm_A_log, ssm_D, ssm_norm_g, ssm_w_out, attn_w_qkv,
           attn_q_norm, attn_k_norm, attn_w_out, final_norm):
    nb_prompt = x_prompt.shape[0]
    x = jnp.concatenate([x_prompt, x_sample], axis=0)
    b, T, _ = x.shape
    n = b * T
    x = x.reshape(n, D_MODEL)
    depth = norm_g.shape[0]

    wg = ffn_w_gate.astype(BF16)
    wu = ffn_w_up.astype(BF16)
    wd = ffn_w_down.astype(BF16)
    fg = final_norm.reshape(1, D_MODEL)
    cos, sin = _rope_tables(T)
    half = jnp.arange(LANES) // ATT_HEAD_DIM
    head_ones = (half[:, None] == half[None, :]).astype(BF16)

    for i in range(depth):
        j = i // 2
        x = _ffn(x, norm_g[i, 0].reshape(1, D_MODEL), wg[i, 0], wu[i, 0], wd[i, 0], fg, False)
        g_mix = norm_g[i, 1].reshape(1, D_MODEL)
        if i % 2 == 0:
            w_in = ssm_w_in[j]
            wz = w_in[:, :D_INNER].astype(BF16)
            wx = w_in[:, D_INNER:D_INNER + CONV_DIM].astype(BF16)
            wdt = w_in[:, D_INNER + CONV_DIM:].reshape(D_MODEL, 2, SSM_GROUPS, HEADS_PER_GROUP)
            wdt = wdt.transpose(0, 2, 1, 3).reshape(D_MODEL, 2 * SSM_HEADS)
            wdt = jnp.pad(wdt, ((0, 0), (0, LANES - 2 * SSM_HEADS))).astype(BF16)
            z, xbc, dt_raw = _ssd_in(x, g_mix, wz, wx, wdt)
            dt_rows = dt_raw[:, :2 * SSM_HEADS].reshape(b, T, SSM_GROUPS, 2 * HEADS_PER_GROUP)
            dt_rows = dt_rows.transpose(0, 2, 3, 1)
            y = _ssd(xbc.reshape(b, T, CONV_DIM), z.reshape(b, T, D_INNER), dt_rows,
                     ssm_conv_w[j], ssm_conv_b[j].reshape(1, CONV_DIM),
                     _group_order(ssm_dt_bias[j])[:, :, None],
                     _group_order(ssm_A_log[j])[:, :, None],
                     jnp.repeat(ssm_D[j], SSM_HEAD_DIM).reshape(SSM_GROUPS, GROUP_DIM, 1),
                     ssm_norm_g[j].reshape(SSM_GROUPS, 1, GROUP_DIM))
            mix = (y.reshape(n, D_INNER), ssm_w_out[j].astype(BF16))
        else:
            q, k, vt = _attn_in(x.reshape(b, T, D_MODEL), g_mix, attn_w_qkv[j].astype(BF16),
                                jnp.tile(attn_q_norm[j], LANES // ATT_HEAD_DIM).reshape(1, LANES),
                                jnp.tile(attn_k_norm[j], LANES // ATT_HEAD_DIM).reshape(1, LANES),
                                cos, sin, head_ones)
            o = _flash(q, k, vt)
            mix = (o.reshape(n, N_HEADS * ATT_HEAD_DIM), attn_w_out[j].astype(BF16))
        x = _ffn(x, norm_g[i, 2].reshape(1, D_MODEL), wg[i, 1], wu[i, 1], wd[i, 1], fg,
                 i == depth - 1, mix)
    x = x.reshape(b, T, D_MODEL)
    return (x[:nb_prompt], x[nb_prompt:])
```

```python
import functools
import math

import jax
import jax.numpy as jnp
from jax import lax
from jax.experimental import pallas as pl
from jax.experimental.pallas import tpu as pltpu

F32 = jnp.float32
BF16 = jnp.bfloat16

D_MODEL = 1024
GRID_W = 64
D_FF = 2816
D_INNER = 2048
SSM_HEAD_DIM = 64
SSM_HEADS = 32
SSM_GROUPS = 8
HEADS_PER_GROUP = 4
GROUP_DIM = HEADS_PER_GROUP * SSM_HEAD_DIM
D_STATE = 128
D_CONV = 5
CONV_DIM = D_INNER + 2 * SSM_GROUPS * D_STATE
ATT_HEAD_DIM = 64
N_HEADS = 16
N_KV_HEADS = 4
KV_REP = 4
AXIS_DIM = 32
ROPE_THETA = 10000.0
EPS = 1e-6

LANES = 128
SUBLANES = 8
VMEM_LIMIT = 56 * 1024 * 1024

TOKEN_TILE = 512
SSD_CHUNK = 128
SSD_TRIP = 4
RELAYOUT_ROWS = 256
IN_SLAB = 256
CONV_ROWS = 64
CONV_HALO = 8
ATT_Q_TILE = 256
ATT_K_TILE = 512
ATT_TRIP = 4
FF_SPLITS = (1024, 1024, 768)

_RESIDENT = dict(pipeline_mode=pl.Buffered(1))


def _params(n_axes):
    return pltpu.CompilerParams(dimension_semantics=("parallel",) * n_axes,
                                vmem_limit_bytes=VMEM_LIMIT)


def _sigmoid(x):
    return 1.0 / (1.0 + jnp.exp(-x))


def _rms(x, g):
    return x * lax.rsqrt(jnp.mean(x * x, axis=-1, keepdims=True) + EPS) * g


def _ffn_kernel(*refs, final, mixed, n_x, n_out, split):
    refs = list(refs)
    x_refs, refs = refs[:n_x], refs[n_x:]
    step = pl.program_id(0)
    if n_x == 2:
        x = jnp.where(step < split, x_refs[0][...], x_refs[1][...])
    else:
        x = x_refs[0][...]
    if mixed:
        (y_ref, wo_ref), refs = refs[:2], refs[2:]
        x = x + jnp.dot(y_ref[...], wo_ref[...], preferred_element_type=F32)
    g_ref, wg_ref, wu_ref, wd_ref, fg_ref = refs[:5]
    o_refs = refs[5:]
    h = _rms(x, g_ref[...]).astype(BF16)
    acc = None
    c0 = 0
    for cw in FF_SPLITS:
        gate = jnp.dot(h, wg_ref[:, c0:c0 + cw], preferred_element_type=F32)
        up = jnp.dot(h, wu_ref[:, c0:c0 + cw], preferred_element_type=F32)
        a = (gate * _sigmoid(gate) * up).astype(BF16)
        part = jnp.dot(a, wd_ref[c0:c0 + cw, :], preferred_element_type=F32)
        acc = part if acc is None else acc + part
        c0 += cw
    y = x + 0.5 * acc
    if final:
        y = _rms(y, fg_ref[...])
    if n_out == 2:
        @pl.when(step < split)
        def _():
            o_refs[0][...] = y

        @pl.when(step >= split)
        def _():
            o_refs[1][...] = y
    else:
        o_refs[0][...] = y


def _ffn(xs, g, wg, wu, wd, fg, final, n_prompt, mix=None, split_out=False):
    xs = xs if isinstance(xs, tuple) else (xs,)
    n = sum(x.shape[0] for x in xs)
    tm = min(TOKEN_TILE, n_prompt)
    split = n_prompt // tm
    row = lambda i: (i, 0)
    first = lambda i: (jnp.minimum(i, split - 1), 0)
    second = lambda i: (jnp.maximum(i - split, 0), 0)
    fixed = lambda i: (0, 0)
    args = list(xs)
    tile = lambda m: pl.BlockSpec((tm, D_MODEL), m)
    in_specs = [tile(first), tile(second)] if len(xs) == 2 else [tile(row)]
    if mix is not None:
        y, wo = mix
        k = y.shape[1]
        args += [y, wo]
        in_specs += [pl.BlockSpec((tm, k), row), pl.BlockSpec((k, D_MODEL), fixed, **_RESIDENT)]
    args += [g, wg, wu, wd, fg]
    in_specs += [pl.BlockSpec((1, D_MODEL), fixed, **_RESIDENT),
                 pl.BlockSpec((D_MODEL, D_FF), fixed, **_RESIDENT),
                 pl.BlockSpec((D_MODEL, D_FF), fixed, **_RESIDENT),
                 pl.BlockSpec((D_FF, D_MODEL), fixed, **_RESIDENT),
                 pl.BlockSpec((1, D_MODEL), fixed, **_RESIDENT)]
    if split_out:
        n_split = split * tm
        out_specs = [tile(first), tile(second)]
        out_shape = [jax.ShapeDtypeStruct((n_split, D_MODEL), F32),
                     jax.ShapeDtypeStruct((n - n_split, D_MODEL), F32)]
    else:
        out_specs = tile(row)
        out_shape = jax.ShapeDtypeStruct((n, D_MODEL), F32)
    return pl.pallas_call(
        functools.partial(_ffn_kernel, final=final, mixed=mix is not None, n_x=len(xs),
                          n_out=2 if split_out else 1, split=split),
        grid=(n // tm,),
        in_specs=in_specs,
        out_specs=out_specs,
        out_shape=out_shape,
        compiler_params=pltpu.CompilerParams(dimension_semantics=("arbitrary",),
                                             vmem_limit_bytes=VMEM_LIMIT),
        name="ffn",
    )(*args)


def _ssd_in_kernel(x_ref, xp_ref, xn_ref, g_ref, wz_ref, wx_ref, wdt_ref, cw_ref, cb_ref,
                   z_ref, xbc_ref, dt_ref, pre_ref):
    t = pl.program_id(1)
    tm = x_ref.shape[1]
    H = CONV_HALO
    g = g_ref[...]
    hm = _rms(x_ref[0], g)
    h = hm.astype(BF16)
    hp = jnp.where(t > 0, _rms(xp_ref[0], g), 0.0)
    hn = jnp.where(t < pl.num_programs(1) - 1, _rms(xn_ref[0], g), 0.0)
    h_ext = jnp.concatenate([hp, hm, hn], axis=0).astype(BF16)
    row0 = jnp.minimum(t, 0)

    def project(s):
        c0 = s * IN_SLAB
        res = jnp.dot(h_ext, wx_ref[:, c0:c0 + IN_SLAB], preferred_element_type=F32)
        for j in range(IN_SLAB // LANES):
            pre_ref[c0 // LANES + j] = res[:, j * LANES:(j + 1) * LANES]

    def conv(s):
        for c0 in range(s * IN_SLAB, (s + 1) * IN_SLAB, LANES):
            for q0 in range(0, tm, CONV_ROWS):
                y = jnp.broadcast_to(cb_ref[:, c0:c0 + LANES], (CONV_ROWS, LANES))
                for k in range(D_CONV):
                    r0 = row0 + (q0 + H - D_CONV // 2 + k)
                    y = y + (cw_ref[k:k + 1, c0:c0 + LANES]
                             * pre_ref[c0 // LANES, pl.ds(r0, CONV_ROWS), :])
                xbc_ref[0, q0:q0 + CONV_ROWS, c0:c0 + LANES] = (y * _sigmoid(y)).astype(BF16)

    n_slab = CONV_DIM // IN_SLAB
    project(0)
    for s in range(1, n_slab):
        project(s)
        conv(s - 1)
    for c0 in range(0, D_INNER, IN_SLAB):
        z_ref[0, :, c0:c0 + IN_SLAB] = jnp.dot(h, wz_ref[:, c0:c0 + IN_SLAB],
                                               preferred_element_type=F32).astype(BF16)
        if c0 == 0:
            conv(n_slab - 1)
    dt_ref[0] = jnp.dot(h, wdt_ref[...], preferred_element_type=F32)


def _ssd_in(x, g, wz, wx, wdt, conv_w, conv_b):
    b, T, _ = x.shape
    tm = min(TOKEN_TILE, T)
    hb = tm // CONV_HALO
    fixed = lambda i, t: (0, 0)
    tile = lambda i, t: (i, t, 0)
    return pl.pallas_call(
        _ssd_in_kernel,
        grid=(b, T // tm),
        in_specs=[pl.BlockSpec((1, tm, D_MODEL), tile),
                  pl.BlockSpec((1, CONV_HALO, D_MODEL),
                               lambda i, t: (i, jnp.maximum(t * hb - 1, 0), 0)),
                  pl.BlockSpec((1, CONV_HALO, D_MODEL),
                               lambda i, t: (i, jnp.minimum((t + 1) * hb, T // CONV_HALO - 1), 0)),
                  pl.BlockSpec((1, D_MODEL), fixed, **_RESIDENT),
                  pl.BlockSpec((D_MODEL, D_INNER), fixed, **_RESIDENT),
                  pl.BlockSpec((D_MODEL, CONV_DIM), fixed, **_RESIDENT),
                  pl.BlockSpec((D_MODEL, LANES), fixed, **_RESIDENT),
                  pl.BlockSpec((D_CONV, CONV_DIM), fixed, **_RESIDENT),
                  pl.BlockSpec((1, CONV_DIM), fixed, **_RESIDENT)],
        out_specs=[pl.BlockSpec((1, tm, D_INNER), tile),
                   pl.BlockSpec((1, tm, CONV_DIM), tile),
                   pl.BlockSpec((1, tm, LANES), tile)],
        out_shape=[jax.ShapeDtypeStruct((b, T, D_INNER), BF16),
                   jax.ShapeDtypeStruct((b, T, CONV_DIM), BF16),
                   jax.ShapeDtypeStruct((b, T, LANES), F32)],
        scratch_shapes=[pltpu.VMEM((CONV_DIM // LANES, tm + 2 * CONV_HALO, LANES), F32)],
        compiler_params=_params(2),
        name="ssd_in",
    )(x, x, x, g, wz, wx, wdt, conv_w, conv_b)


_R_CS, _R_DT, _R_ECS, _R_WEND, _R_ETOT, _R_SPLIT = 0, 8, 16, 24, 32, 40
_N_ROWQ = 64
_NEG = -1e30


def _ssd_kernel(x_ref, b_ref, c_ref, z_ref, dt_ref,
                dtb_ref, alog_ref, d_ref, ng_ref,
                o_ref,
                xt_ref, ct_ref, rowq_ref, cscol_ref, sb_ref, fr_ref, ysb_ref, upd_ref):
    T = x_ref.shape[1]
    Q = SSD_CHUNK
    nc = T // Q
    assert nc % SSD_TRIP == 0 and SSD_TRIP % 2 == 0, (T, SSD_TRIP)
    R = min(RELAYOUT_ROWS, T)
    hp = HEADS_PER_GROUP
    P = SSM_HEAD_DIM

    def relayout(i, carry):
        r0 = pl.multiple_of(i * R, R)
        for s in range(GROUP_DIM // LANES):
            xt_ref[s * LANES:(s + 1) * LANES, pl.ds(r0, R)] = (
                x_ref[0, pl.ds(r0, R), s * LANES:(s + 1) * LANES].astype(F32).T)
        ct_ref[:, pl.ds(r0, R)] = c_ref[0, pl.ds(r0, R), :].astype(F32).T.astype(BF16)
        return carry

    lax.fori_loop(0, T // R, relayout, 0, unroll=2)

    raw = dt_ref[0, 0] + dtb_ref[0]
    dt = jnp.maximum(raw, 0.0) + jnp.log1p(jnp.exp(-jnp.abs(raw)))
    a = dt * (-jnp.exp(alog_ref[0]) * math.log2(math.e))
    lane = lax.broadcasted_iota(jnp.int32, (2 * hp, T), 1) % Q
    is_fwd = lax.broadcasted_iota(jnp.int32, (2 * hp, T), 0) < hp
    pre, suf = a, a
    s = 1
    while s < Q:
        pre = pre + jnp.where(lane >= s, pltpu.roll(pre, s, 1), 0.0)
        suf = suf + jnp.where(lane < Q - s, pltpu.roll(suf, T - s, 1), 0.0)
        s *= 2
    cs = jnp.where(is_fwd, pre, suf)
    rowq_ref[_R_CS:_R_CS + 8, :] = cs
    rowq_ref[_R_DT:_R_DT + 8, :] = dt
    rowq_ref[_R_ECS:_R_ECS + 8, :] = jnp.exp2(cs)
    rowq_ref[_R_WEND:_R_WEND + 8, :] = jnp.exp2(jnp.where(is_fwd, suf, pre) - a) * dt
    rowq_ref[_R_ETOT:_R_ETOT + 8, :] = jnp.exp2(pre + suf - a)
    hi = cs.astype(BF16).astype(F32)
    mid = (cs - hi).astype(BF16).astype(F32)
    rowq_ref[_R_SPLIT:_R_SPLIT + 8, :] = hi
    rowq_ref[_R_SPLIT + 8:_R_SPLIT + 16, :] = mid
    rowq_ref[_R_SPLIT + 16:_R_SPLIT + 24, :] = cs - hi - mid

    def to_cols(c, carry):
        t0 = pl.multiple_of(c * Q, Q)
        tile = jnp.concatenate([rowq_ref[_R_SPLIT:_R_SPLIT + 24, pl.ds(t0, Q)],
                                jnp.zeros((LANES - 24, Q), F32)], axis=0)
        cscol_ref[pl.ds(t0, Q), :] = tile.T.astype(BF16)
        return carry

    lax.fori_loop(0, nc, to_cols, 0, unroll=8)

    def by_head(mat, rows, off):
        return jnp.concatenate([mat[r * P:(r + 1) * P] * rows[off + r:off + r + 1, :]
                                for r in range(hp)], axis=0)

    def state_update(t0, direction):
        xw = by_head(xt_ref[:, pl.ds(t0, Q)], rowq_ref[_R_WEND:_R_WEND + 8, pl.ds(t0, Q)],
                     direction * hp)
        return jnp.dot(xw.astype(BF16), b_ref[0, pl.ds(t0, Q), :], preferred_element_type=F32)

    def state_decay(S, t0, direction):
        return by_head(S, rowq_ref[_R_ETOT:_R_ETOT + 8, pl.ds(t0, Q)], direction * hp)

    def bwd_body(k, S):
        c = nc - 1 - k
        sb_ref[c] = S.astype(BF16)
        t0 = pl.multiple_of(c * Q, Q)
        return state_decay(S, t0, 1) + state_update(t0, 1)

    lax.fori_loop(0, nc, bwd_body, jnp.zeros((GROUP_DIM, D_STATE), F32), unroll=4)

    src = lax.broadcasted_iota(jnp.int32, (Q, Q), 0)
    dst = lax.broadcasted_iota(jnp.int32, (Q, Q), 1)
    causal = dst >= src
    anti = dst <= src
    dmat = jnp.broadcast_to(d_ref[0], (GROUP_DIM, Q))
    sel_k = lax.broadcasted_iota(jnp.int32, (LANES, 2 * hp * Q), 0)
    sel_q = lax.broadcasted_iota(jnp.int32, (LANES, 2 * hp * Q), 1) // Q
    sel = jnp.where((sel_k < 24) & (sel_k % 8 == sel_q), 1.0, 0.0).astype(BF16)
    ng = ng_ref[0]

    def front(c, slot):
        t0 = pl.multiple_of(c * Q, Q)
        ct = ct_ref[:, pl.ds(t0, Q)]
        fr_ref[slot, :, 0:Q] = jnp.dot(b_ref[0, pl.ds(t0, Q), :], ct,
                                       preferred_element_type=F32)
        fr_ref[slot, :, Q:] = jnp.dot(cscol_ref[pl.ds(t0, Q), :], sel,
                                      preferred_element_type=F32)
        ysb_ref[slot] = jnp.dot(sb_ref[c], ct, preferred_element_type=F32)
        upd_ref[slot] = state_update(t0, 0)

    def back(c, slot, S):
        t0 = pl.multiple_of(c * Q, Q)
        ct = ct_ref[:, pl.ds(t0, Q)]
        xt = xt_ref[:, pl.ds(t0, Q)]
        csr = rowq_ref[_R_CS:_R_CS + 8, pl.ds(t0, Q)]
        dtr = rowq_ref[_R_DT:_R_DT + 8, pl.ds(t0, Q)]
        ecs = rowq_ref[_R_ECS:_R_ECS + 8, pl.ds(t0, Q)]
        cbt = fr_ref[slot, :, 0:Q]
        y_sf = jnp.dot(S.astype(BF16), ct, preferred_element_type=F32)
        ys = []
        for r in range(hp):
            cf = fr_ref[slot, :, (1 + r) * Q:(2 + r) * Q]
            cb = fr_ref[slot, :, (1 + hp + r) * Q:(2 + hp + r) * Q]
            mf = cbt * jnp.exp2(jnp.where(causal, csr[r:r + 1, :] - cf, _NEG))
            mb = cbt * jnp.exp2(jnp.where(anti, csr[hp + r:hp + r + 1, :] - cb, _NEG))
            xr = xt[r * P:(r + 1) * P]
            lhs = jnp.concatenate([xr * dtr[r:r + 1, :], xr * dtr[hp + r:hp + r + 1, :]], axis=1)
            rhs = jnp.concatenate([mf, mb], axis=0)
            ys.append(jnp.dot(lhs.astype(BF16), rhs.astype(BF16), preferred_element_type=F32))
        yt = jnp.concatenate(ys, axis=0)
        yt = yt + by_head(y_sf, ecs, 0) + by_head(ysb_ref[slot], ecs, hp) + dmat * xt
        y = yt.T
        zc = z_ref[0, pl.ds(t0, Q), :].astype(F32)
        y = y * (zc * _sigmoid(zc))
        o_ref[0, pl.ds(t0, Q), :] = _rms(y, ng).astype(BF16)
        return state_decay(S, t0, 0) + upd_ref[slot]

    def trip(i, S):
        c = SSD_TRIP * i
        for u in range(SSD_TRIP):
            front(jnp.minimum(c + u + 1, nc - 1), (u + 1) % 2)
            S = back(c + u, u % 2, S)
        return S

    front(0, 0)
    lax.fori_loop(0, nc // SSD_TRIP, trip, jnp.zeros((GROUP_DIM, D_STATE), F32))


def _ssd(xbc, z, dt_rows, dt_bias, a_log, d_vec, norm_g):
    b, T, _ = xbc.shape
    nb = D_INNER // D_STATE
    xblk = lambda i, g: (i, 0, g)
    per_g = lambda i, g: (g, 0, 0)
    return pl.pallas_call(
        _ssd_kernel,
        grid=(b, SSM_GROUPS),
        in_specs=[pl.BlockSpec((1, T, GROUP_DIM), xblk),
                  pl.BlockSpec((1, T, D_STATE), lambda i, g: (i, 0, nb + g)),
                  pl.BlockSpec((1, T, D_STATE), lambda i, g: (i, 0, nb + SSM_GROUPS + g)),
                  pl.BlockSpec((1, T, GROUP_DIM), xblk),
                  pl.BlockSpec((1, 1, 2 * HEADS_PER_GROUP, T), lambda i, g: (i, g, 0, 0)),
                  pl.BlockSpec((1, 2 * HEADS_PER_GROUP, 1), per_g),
                  pl.BlockSpec((1, 2 * HEADS_PER_GROUP, 1), per_g),
                  pl.BlockSpec((1, GROUP_DIM, 1), per_g),
                  pl.BlockSpec((1, 1, GROUP_DIM), per_g)],
        out_specs=pl.BlockSpec((1, T, GROUP_DIM), xblk),
        out_shape=jax.ShapeDtypeStruct((b, T, D_INNER), BF16),
        scratch_shapes=[pltpu.VMEM((GROUP_DIM, T), F32),
                        pltpu.VMEM((D_STATE, T), BF16),
                        pltpu.VMEM((_N_ROWQ, T), F32),
                        pltpu.VMEM((T, LANES), BF16),
                        pltpu.VMEM((T // SSD_CHUNK, GROUP_DIM, D_STATE), BF16),
                        pltpu.VMEM((2, SSD_CHUNK, (1 + 2 * HEADS_PER_GROUP) * SSD_CHUNK), F32),
                        pltpu.VMEM((2, GROUP_DIM, SSD_CHUNK), F32),
                        pltpu.VMEM((2, GROUP_DIM, D_STATE), F32)],
        compiler_params=_params(2),
        name="ssd",
    )(xbc, xbc, xbc, z, dt_rows, dt_bias, a_log, d_vec, norm_g)


def _attn_in_kernel(x_ref, g_ref, w_ref, qg_ref, kg_ref, cos_ref, sin_ref, ones_ref,
                    q_ref, k_ref, vt_ref):
    h = _rms(x_ref[0], g_ref[...]).astype(BF16)
    qkv = jnp.dot(h, w_ref[...], preferred_element_type=F32)
    cos = cos_ref[...]
    sin = sin_ref[...]
    ones = ones_ref[...]
    tm = h.shape[0]
    first_half = (lax.broadcasted_iota(jnp.int32, (tm, LANES), 1) % AXIS_DIM) < AXIS_DIM // 2
    qd = N_HEADS * ATT_HEAD_DIM
    kd = N_KV_HEADS * ATT_HEAD_DIM

    def norm_rope(v, gain):
        sq = v * v
        hi = sq.astype(BF16)
        lo = (sq - hi.astype(F32)).astype(BF16)
        ss = (jnp.dot(hi, ones, preferred_element_type=F32)
              + jnp.dot(lo, ones, preferred_element_type=F32))
        n = v * lax.rsqrt(ss * (1.0 / ATT_HEAD_DIM) + EPS) * gain
        partner = jnp.where(first_half,
                            pltpu.roll(n, LANES - AXIS_DIM // 2, 1),
                            pltpu.roll(n, AXIS_DIM // 2, 1))
        return n * cos + partner * sin

    qgain = qg_ref[...] * (ATT_HEAD_DIM ** -0.5 * math.log2(math.e))
    for s in range(qd // LANES):
        q_ref[0, :, s * LANES:(s + 1) * LANES] = norm_rope(
            qkv[:, s * LANES:(s + 1) * LANES], qgain).astype(BF16)
    for s in range(kd // LANES):
        kk = norm_rope(qkv[:, qd + s * LANES:qd + (s + 1) * LANES], kg_ref[...]).astype(BF16)
        k_ref[0, 2 * s] = kk[:, :ATT_HEAD_DIM]
        k_ref[0, 2 * s + 1] = kk[:, ATT_HEAD_DIM:]
    vt_ref[0] = qkv[:, qd + kd:].T.astype(BF16)


def _attn_in(x, g, w, qg, kg, cos, sin, ones):
    b, T, _ = x.shape
    tm = min(TOKEN_TILE, T)
    qkv_dim = (N_HEADS + 2 * N_KV_HEADS) * ATT_HEAD_DIM
    fixed = lambda i, t: (0, 0)
    return pl.pallas_call(
        _attn_in_kernel,
        grid=(b, T // tm),
        in_specs=[pl.BlockSpec((1, tm, D_MODEL), lambda i, t: (i, t, 0)),
                  pl.BlockSpec((1, D_MODEL), fixed, **_RESIDENT),
                  pl.BlockSpec((D_MODEL, qkv_dim), fixed, **_RESIDENT),
                  pl.BlockSpec((1, LANES), fixed, **_RESIDENT),
                  pl.BlockSpec((1, LANES), fixed, **_RESIDENT),
                  pl.BlockSpec((tm, LANES), lambda i, t: (t, 0)),
                  pl.BlockSpec((tm, LANES), lambda i, t: (t, 0)),
                  pl.BlockSpec((LANES, LANES), fixed, **_RESIDENT)],
        out_specs=[pl.BlockSpec((1, tm, N_HEADS * ATT_HEAD_DIM), lambda i, t: (i, t, 0)),
                   pl.BlockSpec((1, N_KV_HEADS, tm, ATT_HEAD_DIM), lambda i, t: (i, 0, t, 0)),
                   pl.BlockSpec((1, N_KV_HEADS * ATT_HEAD_DIM, tm), lambda i, t: (i, 0, t))],
        out_shape=[jax.ShapeDtypeStruct((b, T, N_HEADS * ATT_HEAD_DIM), BF16),
                   jax.ShapeDtypeStruct((b, N_KV_HEADS, T, ATT_HEAD_DIM), BF16),
                   jax.ShapeDtypeStruct((b, N_KV_HEADS * ATT_HEAD_DIM, T), BF16)],
        compiler_params=_params(2),
        name="attn_in",
    )(x, g, w, qg, kg, cos, sin, ones)


def _flash_kernel(q_ref, k_ref, vt_ref, o_ref, s_ref):
    tq = q_ref.shape[1]
    T = k_ref.shape[2]
    tk = s_ref.shape[1]
    dh = ATT_HEAD_DIM
    ones = jnp.ones((2 * SUBLANES, tk), BF16)
    nq = KV_REP * tq
    nk = T // tk

    def scores(g, c, slot):
        q4 = jnp.concatenate([q_ref[0, :, (g * KV_REP + r) * dh:(g * KV_REP + r + 1) * dh]
                              for r in range(KV_REP)], axis=0)
        k0 = pl.multiple_of(c * tk, tk)
        kc = k_ref[0, g, pl.ds(k0, tk), :]
        s_ref[slot] = lax.dot_general(kc, q4, (((1,), (1,)), ((), ())),
                                      preferred_element_type=F32)

    def attend(g, c, slot, m, acc):
        k0 = pl.multiple_of(c * tk, tk)
        va = jnp.concatenate([vt_ref[0, g * dh:(g + 1) * dh, pl.ds(k0, tk)], ones], axis=0)
        s = s_ref[slot]
        m_new = jnp.maximum(m, jnp.max(s, axis=0, keepdims=True))
        p = jnp.exp2(s - m_new).astype(BF16)
        acc = jnp.exp2(m - m_new) * acc + jnp.dot(va, p, preferred_element_type=F32)
        return m_new, acc

    scores(0, 0, 0)
    per_trip = min(ATT_TRIP, nk)
    for g in range(N_KV_HEADS):

        def body(i, carry, g=g):
            m, acc = carry
            c = per_trip * i
            for u in range(per_trip):
                scores(g, c + u + 1, (u + 1) % 2)
                m, acc = attend(g, c + u, u % 2, m, acc)
            return m, acc

        init = (jnp.full((1, nq), -jnp.inf, F32), jnp.zeros((dh + 2 * SUBLANES, nq), F32))
        m, acc = lax.fori_loop(0, nk // per_trip - 1, body, init)
        for u in range(per_trip):
            c = nk - per_trip + u
            if c + 1 < nk:
                scores(g, c + 1, (u + 1) % 2)
            elif g + 1 < N_KV_HEADS:
                scores(g + 1, 0, 0)
            m, acc = attend(g, c, u % 2, m, acc)
        o4 = acc[:dh] / acc[dh:dh + 1]
        ot = jnp.concatenate([o4[:, r * tq:(r + 1) * tq] for r in range(KV_REP)], axis=0)
        o_ref[0, :, g * KV_REP * dh:(g + 1) * KV_REP * dh] = ot.T.astype(BF16)


def _flash(q, k, vt):
    b, T, qd = q.shape
    tq = min(ATT_Q_TILE, T)
    tk = min(ATT_K_TILE, T // 2)
    per_trip = min(ATT_TRIP, T // tk)
    assert T % tk == 0 and per_trip % 2 == 0 and (T // tk) % per_trip == 0, (T, tk, per_trip)
    return pl.pallas_call(
        _flash_kernel,
        grid=(b, T // tq),
        in_specs=[pl.BlockSpec((1, tq, qd), lambda i, t: (i, t, 0)),
                  pl.BlockSpec((1, N_KV_HEADS, T, ATT_HEAD_DIM), lambda i, t: (i, 0, 0, 0)),
                  pl.BlockSpec((1, N_KV_HEADS * ATT_HEAD_DIM, T), lambda i, t: (i, 0, 0))],
        out_specs=pl.BlockSpec((1, tq, qd), lambda i, t: (i, t, 0)),
        out_shape=jax.ShapeDtypeStruct((b, T, qd), BF16),
        scratch_shapes=[pltpu.VMEM((2, tk, KV_REP * tq), F32)],
        compiler_params=_params(2),
        name="flash",
    )(q, k, vt)


def _rope_tables(T):
    t = jnp.arange(T, dtype=jnp.int32)
    pos = jnp.stack([(t // GRID_W).astype(F32), (t % GRID_W).astype(F32)], axis=1)
    inv_freq = ROPE_THETA ** (-jnp.arange(0, AXIS_DIM, 2, dtype=F32) / AXIS_DIM)
    ang = pos[:, :, None] * inv_freq[None, None, :]
    cos = jnp.cos(ang)
    sin = jnp.sin(ang)
    cos_d = jnp.concatenate([cos, cos], axis=-1).reshape(T, ATT_HEAD_DIM)
    sin_d = jnp.concatenate([-sin, sin], axis=-1).reshape(T, ATT_HEAD_DIM)
    return jnp.tile(cos_d, (1, LANES // ATT_HEAD_DIM)), jnp.tile(sin_d, (1, LANES // ATT_HEAD_DIM))


def _group_order(v):
    return v.reshape(2, SSM_GROUPS, HEADS_PER_GROUP).transpose(1, 0, 2).reshape(SSM_GROUPS, 2 * HEADS_PER_GROUP)


def kernel(x_prompt, x_sample, norm_g, ffn_w_gate, ffn_w_up, ffn_w_down, ssm_w_in, ssm_conv_w,
           ssm_conv_b, ssm_dt_bias, ssm_A_log, ssm_D, ssm_norm_g, ssm_w_out, attn_w_qkv,
           attn_q_norm, attn_k_norm, attn_w_out, final_norm):
    nb_prompt, T, _ = x_prompt.shape
    b = nb_prompt + x_sample.shape[0]
    n = b * T
    x = (x_prompt.reshape(nb_prompt * T, D_MODEL), x_sample.reshape(n - nb_prompt * T, D_MODEL))
    depth = norm_g.shape[0]

    wg = ffn_w_gate.astype(BF16)
    wu = ffn_w_up.astype(BF16)
    wd = ffn_w_down.astype(BF16)
    fg = final_norm.reshape(1, D_MODEL)
    cos, sin = _rope_tables(T)
    half = jnp.arange(LANES) // ATT_HEAD_DIM
    head_ones = (half[:, None] == half[None, :]).astype(BF16)

    for i in range(depth):
        j = i // 2
        x = _ffn(x, norm_g[i, 0].reshape(1, D_MODEL), wg[i, 0], wu[i, 0], wd[i, 0], fg, False,
                 nb_prompt * T)
        g_mix = norm_g[i, 1].reshape(1, D_MODEL)
        if i % 2 == 0:
            w_in = ssm_w_in[j]
            wz = w_in[:, :D_INNER].astype(BF16)
            wx = w_in[:, D_INNER:D_INNER + CONV_DIM].astype(BF16)
            wdt = w_in[:, D_INNER + CONV_DIM:].reshape(D_MODEL, 2, SSM_GROUPS, HEADS_PER_GROUP)
            wdt = wdt.transpose(0, 2, 1, 3).reshape(D_MODEL, 2 * SSM_HEADS)
            wdt = jnp.pad(wdt, ((0, 0), (0, LANES - 2 * SSM_HEADS))).astype(BF16)
            z, xbc, dt_raw = _ssd_in(x.reshape(b, T, D_MODEL), g_mix, wz, wx, wdt,
                                     ssm_conv_w[j], ssm_conv_b[j].reshape(1, CONV_DIM))
            dt_rows = dt_raw[:, :, :2 * SSM_HEADS].reshape(b, T, SSM_GROUPS, 2 * HEADS_PER_GROUP)
            dt_rows = dt_rows.transpose(0, 2, 3, 1)
            y = _ssd(xbc, z, dt_rows,
                     _group_order(ssm_dt_bias[j])[:, :, None],
                     _group_order(ssm_A_log[j])[:, :, None],
                     jnp.repeat(ssm_D[j], SSM_HEAD_DIM).reshape(SSM_GROUPS, GROUP_DIM, 1),
                     ssm_norm_g[j].reshape(SSM_GROUPS, 1, GROUP_DIM))
            mix = (y.reshape(n, D_INNER), ssm_w_out[j].astype(BF16))
        else:
            q, k, vt = _attn_in(x.reshape(b, T, D_MODEL), g_mix, attn_w_qkv[j].astype(BF16),
                                jnp.tile(attn_q_norm[j], LANES // ATT_HEAD_DIM).reshape(1, LANES),
                                jnp.tile(attn_k_norm[j], LANES // ATT_HEAD_DIM).reshape(1, LANES),
                                cos, sin, head_ones)
            o = _flash(q, k, vt)
            mix = (o.reshape(n, N_HEADS * ATT_HEAD_DIM), attn_w_out[j].astype(BF16))
        last = i == depth - 1
        x = _ffn(x, norm_g[i, 2].reshape(1, D_MODEL), wg[i, 1], wu[i, 1], wd[i, 1], fg,
                 last, nb_prompt * T, mix, split_out=last)
    y_prompt, y_sample = x
    return (y_prompt.reshape(x_prompt.shape), y_sample.reshape(x_sample.shape))
```

```python
import functools
import math

import jax
import jax.numpy as jnp
from jax import lax
from jax.experimental import pallas as pl
from jax.experimental.pallas import tpu as pltpu

F32 = jnp.float32
BF16 = jnp.bfloat16

D_MODEL = 1024
GRID_W = 64
D_FF = 2816
D_INNER = 2048
SSM_HEAD_DIM = 64
SSM_HEADS = 32
SSM_GROUPS = 8
HEADS_PER_GROUP = 4
GROUP_DIM = HEADS_PER_GROUP * SSM_HEAD_DIM
D_STATE = 128
D_CONV = 5
CONV_DIM = D_INNER + 2 * SSM_GROUPS * D_STATE
ATT_HEAD_DIM = 64
N_HEADS = 16
N_KV_HEADS = 4
KV_REP = 4
AXIS_DIM = 32
ROPE_THETA = 10000.0
EPS = 1e-6

LANES = 128
SUBLANES = 8
VMEM_LIMIT = 56 * 1024 * 1024

TOKEN_TILE = 512
SSD_CHUNK = 128
SSD_TRIP = 4
RELAYOUT_ROWS = 256
IN_SLAB = 256
CONV_ROWS = 64
CONV_HALO = 8
ATT_Q_TILE = 256
ATT_K_TILE = 512
ATT_TRIP = 4
SHIFT_SAFE_LOG2 = 100.0
FF_SPLITS = (1024, 1024, 768)

_RESIDENT = dict(pipeline_mode=pl.Buffered(1))


def _params(n_axes):
    return pltpu.CompilerParams(dimension_semantics=("parallel",) * n_axes,
                                vmem_limit_bytes=VMEM_LIMIT)


def _sigmoid(x):
    return 1.0 / (1.0 + jnp.exp(-x))


def _rms(x, g):
    return x * lax.rsqrt(jnp.mean(x * x, axis=-1, keepdims=True) + EPS) * g


def _ffn_kernel(*refs, final, mixed, n_x, n_out, split):
    refs = list(refs)
    x_refs, refs = refs[:n_x], refs[n_x:]
    step = pl.program_id(0)
    if n_x == 2:
        x = jnp.where(step < split, x_refs[0][...], x_refs[1][...])
    else:
        x = x_refs[0][...]
    if mixed:
        (y_ref, wo_ref), refs = refs[:2], refs[2:]
        x = x + jnp.dot(y_ref[...], wo_ref[...], preferred_element_type=F32)
    g_ref, wg_ref, wu_ref, wd_ref, fg_ref = refs[:5]
    o_refs = refs[5:]
    h = _rms(x, g_ref[...]).astype(BF16)
    acc = None
    c0 = 0
    for cw in FF_SPLITS:
        gate = jnp.dot(h, wg_ref[:, c0:c0 + cw], preferred_element_type=F32)
        up = jnp.dot(h, wu_ref[:, c0:c0 + cw], preferred_element_type=F32)
        a = (gate * _sigmoid(gate) * up).astype(BF16)
        part = jnp.dot(a, wd_ref[c0:c0 + cw, :], preferred_element_type=F32)
        acc = part if acc is None else acc + part
        c0 += cw
    y = x + 0.5 * acc
    if final:
        y = _rms(y, fg_ref[...])
    if n_out == 2:
        @pl.when(step < split)
        def _():
            o_refs[0][...] = y

        @pl.when(step >= split)
        def _():
            o_refs[1][...] = y
    else:
        o_refs[0][...] = y


def _ffn(xs, g, wg, wu, wd, fg, final, n_prompt, mix=None, split_out=False):
    xs = xs if isinstance(xs, tuple) else (xs,)
    n = sum(x.shape[0] for x in xs)
    tm = min(TOKEN_TILE, n_prompt)
    split = n_prompt // tm
    row = lambda i: (i, 0)
    first = lambda i: (jnp.minimum(i, split - 1), 0)
    second = lambda i: (jnp.maximum(i - split, 0), 0)
    fixed = lambda i: (0, 0)
    args = list(xs)
    tile = lambda m: pl.BlockSpec((tm, D_MODEL), m)
    in_specs = [tile(first), tile(second)] if len(xs) == 2 else [tile(row)]
    if mix is not None:
        y, wo = mix
        k = y.shape[1]
        args += [y, wo]
        in_specs += [pl.BlockSpec((tm, k), row), pl.BlockSpec((k, D_MODEL), fixed, **_RESIDENT)]
    args += [g, wg, wu, wd, fg]
    in_specs += [pl.BlockSpec((1, D_MODEL), fixed, **_RESIDENT),
                 pl.BlockSpec((D_MODEL, D_FF), fixed, **_RESIDENT),
                 pl.BlockSpec((D_MODEL, D_FF), fixed, **_RESIDENT),
                 pl.BlockSpec((D_FF, D_MODEL), fixed, **_RESIDENT),
                 pl.BlockSpec((1, D_MODEL), fixed, **_RESIDENT)]
    if split_out:
        n_split = split * tm
        out_specs = [tile(first), tile(second)]
        out_shape = [jax.ShapeDtypeStruct((n_split, D_MODEL), F32),
                     jax.ShapeDtypeStruct((n - n_split, D_MODEL), F32)]
    else:
        out_specs = tile(row)
        out_shape = jax.ShapeDtypeStruct((n, D_MODEL), F32)
    return pl.pallas_call(
        functools.partial(_ffn_kernel, final=final, mixed=mix is not None, n_x=len(xs),
                          n_out=2 if split_out else 1, split=split),
        grid=(n // tm,),
        in_specs=in_specs,
        out_specs=out_specs,
        out_shape=out_shape,
        compiler_params=pltpu.CompilerParams(dimension_semantics=("arbitrary",),
                                             vmem_limit_bytes=VMEM_LIMIT),
        name="ffn",
    )(*args)


def _ssd_in_kernel(x_ref, xp_ref, xn_ref, g_ref, wz_ref, wx_ref, wdt_ref, cw_ref, cb_ref,
                   z_ref, xbc_ref, dt_ref, pre_ref):
    t = pl.program_id(1)
    tm = x_ref.shape[1]
    H = CONV_HALO
    g = g_ref[...]
    hm = _rms(x_ref[0], g)
    h = hm.astype(BF16)
    hp = jnp.where(t > 0, _rms(xp_ref[0], g), 0.0)
    hn = jnp.where(t < pl.num_programs(1) - 1, _rms(xn_ref[0], g), 0.0)
    h_ext = jnp.concatenate([hp, hm, hn], axis=0).astype(BF16)
    row0 = jnp.minimum(t, 0)

    def project(s):
        c0 = s * IN_SLAB
        res = jnp.dot(h_ext, wx_ref[:, c0:c0 + IN_SLAB], preferred_element_type=F32)
        for j in range(IN_SLAB // LANES):
            pre_ref[c0 // LANES + j] = res[:, j * LANES:(j + 1) * LANES]

    def conv(s):
        for c0 in range(s * IN_SLAB, (s + 1) * IN_SLAB, LANES):
            for q0 in range(0, tm, CONV_ROWS):
                y = jnp.broadcast_to(cb_ref[:, c0:c0 + LANES], (CONV_ROWS, LANES))
                for k in range(D_CONV):
                    r0 = row0 + (q0 + H - D_CONV // 2 + k)
                    y = y + (cw_ref[k:k + 1, c0:c0 + LANES]
                             * pre_ref[c0 // LANES, pl.ds(r0, CONV_ROWS), :])
                xbc_ref[0, q0:q0 + CONV_ROWS, c0:c0 + LANES] = (y * _sigmoid(y)).astype(BF16)

    n_slab = CONV_DIM // IN_SLAB
    project(0)
    for s in range(1, n_slab):
        project(s)
        conv(s - 1)
    for c0 in range(0, D_INNER, IN_SLAB):
        z_ref[0, :, c0:c0 + IN_SLAB] = jnp.dot(h, wz_ref[:, c0:c0 + IN_SLAB],
                                               preferred_element_type=F32).astype(BF16)
        if c0 == 0:
            conv(n_slab - 1)
    dt_ref[0] = jnp.dot(h, wdt_ref[...], preferred_element_type=F32)


def _ssd_in(x, g, wz, wx, wdt, conv_w, conv_b):
    b, T, _ = x.shape
    tm = min(TOKEN_TILE, T)
    hb = tm // CONV_HALO
    fixed = lambda i, t: (0, 0)
    tile = lambda i, t: (i, t, 0)
    return pl.pallas_call(
        _ssd_in_kernel,
        grid=(b, T // tm),
        in_specs=[pl.BlockSpec((1, tm, D_MODEL), tile),
                  pl.BlockSpec((1, CONV_HALO, D_MODEL),
                               lambda i, t: (i, jnp.maximum(t * hb - 1, 0), 0)),
                  pl.BlockSpec((1, CONV_HALO, D_MODEL),
                               lambda i, t: (i, jnp.minimum((t + 1) * hb, T // CONV_HALO - 1), 0)),
                  pl.BlockSpec((1, D_MODEL), fixed, **_RESIDENT),
                  pl.BlockSpec((D_MODEL, D_INNER), fixed, **_RESIDENT),
                  pl.BlockSpec((D_MODEL, CONV_DIM), fixed, **_RESIDENT),
                  pl.BlockSpec((D_MODEL, LANES), fixed, **_RESIDENT),
                  pl.BlockSpec((D_CONV, CONV_DIM), fixed, **_RESIDENT),
                  pl.BlockSpec((1, CONV_DIM), fixed, **_RESIDENT)],
        out_specs=[pl.BlockSpec((1, tm, D_INNER), tile),
                   pl.BlockSpec((1, tm, CONV_DIM), tile),
                   pl.BlockSpec((1, tm, LANES), tile)],
        out_shape=[jax.ShapeDtypeStruct((b, T, D_INNER), BF16),
                   jax.ShapeDtypeStruct((b, T, CONV_DIM), BF16),
                   jax.ShapeDtypeStruct((b, T, LANES), F32)],
        scratch_shapes=[pltpu.VMEM((CONV_DIM // LANES, tm + 2 * CONV_HALO, LANES), F32)],
        compiler_params=_params(2),
        name="ssd_in",
    )(x, x, x, g, wz, wx, wdt, conv_w, conv_b)


_R_CS, _R_DT, _R_ECS, _R_WEND, _R_ETOT, _R_SPLIT = 0, 8, 16, 24, 32, 40
_N_ROWQ = 64
_NEG = -1e30


def _ssd_kernel(x_ref, b_ref, c_ref, z_ref, dt_ref,
                dtb_ref, alog_ref, d_ref, ng_ref,
                o_ref,
                xt_ref, ct_ref, rowq_ref, cscol_ref, sb_ref, fr_ref, ysb_ref, upd_ref):
    T = x_ref.shape[1]
    Q = SSD_CHUNK
    nc = T // Q
    assert nc % SSD_TRIP == 0 and SSD_TRIP % 2 == 0, (T, SSD_TRIP)
    R = min(RELAYOUT_ROWS, T)
    hp = HEADS_PER_GROUP
    P = SSM_HEAD_DIM

    def relayout(i, carry):
        r0 = pl.multiple_of(i * R, R)
        for s in range(GROUP_DIM // LANES):
            xt_ref[s * LANES:(s + 1) * LANES, pl.ds(r0, R)] = (
                x_ref[0, pl.ds(r0, R), s * LANES:(s + 1) * LANES].astype(F32).T)
        ct_ref[:, pl.ds(r0, R)] = c_ref[0, pl.ds(r0, R), :].astype(F32).T.astype(BF16)
        return carry

    lax.fori_loop(0, T // R, relayout, 0, unroll=2)

    raw = dt_ref[0, 0] + dtb_ref[0]
    dt = jnp.maximum(raw, 0.0) + jnp.log1p(jnp.exp(-jnp.abs(raw)))
    a = dt * (-jnp.exp(alog_ref[0]) * math.log2(math.e))
    lane = lax.broadcasted_iota(jnp.int32, (2 * hp, T), 1) % Q
    is_fwd = lax.broadcasted_iota(jnp.int32, (2 * hp, T), 0) < hp
    pre, suf = a, a
    s = 1
    while s < Q:
        pre = pre + jnp.where(lane >= s, pltpu.roll(pre, s, 1), 0.0)
        suf = suf + jnp.where(lane < Q - s, pltpu.roll(suf, T - s, 1), 0.0)
        s *= 2
    cs = jnp.where(is_fwd, pre, suf)
    rowq_ref[_R_CS:_R_CS + 8, :] = cs
    rowq_ref[_R_DT:_R_DT + 8, :] = dt
    rowq_ref[_R_ECS:_R_ECS + 8, :] = jnp.exp2(cs)
    rowq_ref[_R_WEND:_R_WEND + 8, :] = jnp.exp2(jnp.where(is_fwd, suf, pre) - a) * dt
    rowq_ref[_R_ETOT:_R_ETOT + 8, :] = jnp.exp2(pre + suf - a)
    hi = cs.astype(BF16).astype(F32)
    mid = (cs - hi).astype(BF16).astype(F32)
    rowq_ref[_R_SPLIT:_R_SPLIT + 8, :] = hi
    rowq_ref[_R_SPLIT + 8:_R_SPLIT + 16, :] = mid
    rowq_ref[_R_SPLIT + 16:_R_SPLIT + 24, :] = cs - hi - mid

    def to_cols(c, carry):
        t0 = pl.multiple_of(c * Q, Q)
        tile = jnp.concatenate([rowq_ref[_R_SPLIT:_R_SPLIT + 24, pl.ds(t0, Q)],
                                jnp.zeros((LANES - 24, Q), F32)], axis=0)
        cscol_ref[pl.ds(t0, Q), :] = tile.T.astype(BF16)
        return carry

    lax.fori_loop(0, nc, to_cols, 0, unroll=8)

    def by_head(mat, rows, off):
        return jnp.concatenate([mat[r * P:(r + 1) * P] * rows[off + r:off + r + 1, :]
                                for r in range(hp)], axis=0)

    def state_update(t0, direction):
        xw = by_head(xt_ref[:, pl.ds(t0, Q)], rowq_ref[_R_WEND:_R_WEND + 8, pl.ds(t0, Q)],
                     direction * hp)
        return jnp.dot(xw.astype(BF16), b_ref[0, pl.ds(t0, Q), :], preferred_element_type=F32)

    def state_decay(S, t0, direction):
        return by_head(S, rowq_ref[_R_ETOT:_R_ETOT + 8, pl.ds(t0, Q)], direction * hp)

    def bwd_body(k, S):
        c = nc - 1 - k
        sb_ref[c] = S.astype(BF16)
        t0 = pl.multiple_of(c * Q, Q)
        return state_decay(S, t0, 1) + state_update(t0, 1)

    lax.fori_loop(0, nc, bwd_body, jnp.zeros((GROUP_DIM, D_STATE), F32), unroll=4)

    src = lax.broadcasted_iota(jnp.int32, (Q, Q), 0)
    dst = lax.broadcasted_iota(jnp.int32, (Q, Q), 1)
    causal = dst >= src
    anti = dst <= src
    dmat = jnp.broadcast_to(d_ref[0], (GROUP_DIM, Q))
    sel_k = lax.broadcasted_iota(jnp.int32, (LANES, 2 * hp * Q), 0)
    sel_q = lax.broadcasted_iota(jnp.int32, (LANES, 2 * hp * Q), 1) // Q
    sel = jnp.where((sel_k < 24) & (sel_k % 8 == sel_q), 1.0, 0.0).astype(BF16)
    ng = ng_ref[0]

    def front(c, slot):
        t0 = pl.multiple_of(c * Q, Q)
        ct = ct_ref[:, pl.ds(t0, Q)]
        fr_ref[slot, :, 0:Q] = jnp.dot(b_ref[0, pl.ds(t0, Q), :], ct,
                                       preferred_element_type=F32)
        fr_ref[slot, :, Q:] = jnp.dot(cscol_ref[pl.ds(t0, Q), :], sel,
                                      preferred_element_type=F32)
        ysb_ref[slot] = jnp.dot(sb_ref[c], ct, preferred_element_type=F32)
        upd_ref[slot] = state_update(t0, 0)

    def back(c, slot, S):
        t0 = pl.multiple_of(c * Q, Q)
        ct = ct_ref[:, pl.ds(t0, Q)]
        xt = xt_ref[:, pl.ds(t0, Q)]
        csr = rowq_ref[_R_CS:_R_CS + 8, pl.ds(t0, Q)]
        dtr = rowq_ref[_R_DT:_R_DT + 8, pl.ds(t0, Q)]
        ecs = rowq_ref[_R_ECS:_R_ECS + 8, pl.ds(t0, Q)]
        cbt = fr_ref[slot, :, 0:Q]
        y_sf = jnp.dot(S.astype(BF16), ct, preferred_element_type=F32)
        ys = []
        for r in range(hp):
            cf = fr_ref[slot, :, (1 + r) * Q:(2 + r) * Q]
            cb = fr_ref[slot, :, (1 + hp + r) * Q:(2 + hp + r) * Q]
            mf = cbt * jnp.exp2(jnp.where(causal, csr[r:r + 1, :] - cf, _NEG))
            mb = cbt * jnp.exp2(jnp.where(anti, csr[hp + r:hp + r + 1, :] - cb, _NEG))
            xr = xt[r * P:(r + 1) * P]
            lhs = jnp.concatenate([xr * dtr[r:r + 1, :], xr * dtr[hp + r:hp + r + 1, :]], axis=1)
            rhs = jnp.concatenate([mf, mb], axis=0)
            ys.append(jnp.dot(lhs.astype(BF16), rhs.astype(BF16), preferred_element_type=F32))
        yt = jnp.concatenate(ys, axis=0)
        yt = yt + by_head(y_sf, ecs, 0) + by_head(ysb_ref[slot], ecs, hp) + dmat * xt
        y = yt.T
        zc = z_ref[0, pl.ds(t0, Q), :].astype(F32)
        y = y * (zc * _sigmoid(zc))
        o_ref[0, pl.ds(t0, Q), :] = _rms(y, ng).astype(BF16)
        return state_decay(S, t0, 0) + upd_ref[slot]

    def trip(i, S):
        c = SSD_TRIP * i
        for u in range(SSD_TRIP):
            front(jnp.minimum(c + u + 1, nc - 1), (u + 1) % 2)
            S = back(c + u, u % 2, S)
        return S

    front(0, 0)
    lax.fori_loop(0, nc // SSD_TRIP, trip, jnp.zeros((GROUP_DIM, D_STATE), F32))


def _ssd(xbc, z, dt_rows, dt_bias, a_log, d_vec, norm_g):
    b, T, _ = xbc.shape
    nb = D_INNER // D_STATE
    xblk = lambda i, g: (i, 0, g)
    per_g = lambda i, g: (g, 0, 0)
    return pl.pallas_call(
        _ssd_kernel,
        grid=(b, SSM_GROUPS),
        in_specs=[pl.BlockSpec((1, T, GROUP_DIM), xblk),
                  pl.BlockSpec((1, T, D_STATE), lambda i, g: (i, 0, nb + g)),
                  pl.BlockSpec((1, T, D_STATE), lambda i, g: (i, 0, nb + SSM_GROUPS + g)),
                  pl.BlockSpec((1, T, GROUP_DIM), xblk),
                  pl.BlockSpec((1, 1, 2 * HEADS_PER_GROUP, T), lambda i, g: (i, g, 0, 0)),
                  pl.BlockSpec((1, 2 * HEADS_PER_GROUP, 1), per_g),
                  pl.BlockSpec((1, 2 * HEADS_PER_GROUP, 1), per_g),
                  pl.BlockSpec((1, GROUP_DIM, 1), per_g),
                  pl.BlockSpec((1, 1, GROUP_DIM), per_g)],
        out_specs=pl.BlockSpec((1, T, GROUP_DIM), xblk),
        out_shape=jax.ShapeDtypeStruct((b, T, D_INNER), BF16),
        scratch_shapes=[pltpu.VMEM((GROUP_DIM, T), F32),
                        pltpu.VMEM((D_STATE, T), BF16),
                        pltpu.VMEM((_N_ROWQ, T), F32),
                        pltpu.VMEM((T, LANES), BF16),
                        pltpu.VMEM((T // SSD_CHUNK, GROUP_DIM, D_STATE), BF16),
                        pltpu.VMEM((2, SSD_CHUNK, (1 + 2 * HEADS_PER_GROUP) * SSD_CHUNK), F32),
                        pltpu.VMEM((2, GROUP_DIM, SSD_CHUNK), F32),
                        pltpu.VMEM((2, GROUP_DIM, D_STATE), F32)],
        compiler_params=_params(2),
        name="ssd",
    )(xbc, xbc, xbc, z, dt_rows, dt_bias, a_log, d_vec, norm_g)


def _attn_in_kernel(x_ref, g_ref, w_ref, qg_ref, kg_ref, cos_ref, sin_ref, ones_ref,
                    q_ref, k_ref, vt_ref, *, shifted):
    h = _rms(x_ref[0], g_ref[...]).astype(BF16)
    qkv = jnp.dot(h, w_ref[...], preferred_element_type=F32)
    cos = cos_ref[...]
    sin = sin_ref[...]
    ones = ones_ref[...]
    tm = h.shape[0]
    first_half = (lax.broadcasted_iota(jnp.int32, (tm, LANES), 1) % AXIS_DIM) < AXIS_DIM // 2
    qd = N_HEADS * ATT_HEAD_DIM
    kd = N_KV_HEADS * ATT_HEAD_DIM

    def norm_rope(v, gain):
        sq = v * v
        hi = sq.astype(BF16)
        lo = (sq - hi.astype(F32)).astype(BF16)
        ss = (jnp.dot(hi, ones, preferred_element_type=F32)
              + jnp.dot(lo, ones, preferred_element_type=F32))
        n = v * lax.rsqrt(ss * (1.0 / ATT_HEAD_DIM) + EPS) * gain
        partner = jnp.where(first_half,
                            pltpu.roll(n, LANES - AXIS_DIM // 2, 1),
                            pltpu.roll(n, AXIS_DIM // 2, 1))
        return n * cos + partner * sin

    qgain = qg_ref[...] * (ATT_HEAD_DIM ** -0.5 * math.log2(math.e))
    lane = lax.broadcasted_iota(jnp.int32, (tm, LANES), 1)
    bound = _score_bound(qg_ref[...], kg_ref[...])

    def padded(v, extra):
        tail = jnp.where(lane == ATT_HEAD_DIM, extra, 0.0)
        return (jnp.where(lane < ATT_HEAD_DIM, v, tail).astype(BF16),
                jnp.where(lane < ATT_HEAD_DIM, pltpu.roll(v, ATT_HEAD_DIM, 1), tail).astype(BF16))

    for s in range(qd // LANES):
        qq = norm_rope(qkv[:, s * LANES:(s + 1) * LANES], qgain)
        if shifted:
            qa, qb = padded(qq, 1.0)
            q_ref[0, :, 2 * s * LANES:(2 * s + 1) * LANES] = qa
            q_ref[0, :, (2 * s + 1) * LANES:(2 * s + 2) * LANES] = qb
        else:
            q_ref[0, :, s * LANES:(s + 1) * LANES] = qq.astype(BF16)
    for s in range(kd // LANES):
        kk = norm_rope(qkv[:, qd + s * LANES:qd + (s + 1) * LANES], kg_ref[...])
        if shifted:
            k_ref[0, 2 * s], k_ref[0, 2 * s + 1] = padded(kk, -bound)
        else:
            kk = kk.astype(BF16)
            k_ref[0, 2 * s] = kk[:, :ATT_HEAD_DIM]
            k_ref[0, 2 * s + 1] = kk[:, ATT_HEAD_DIM:]
    vt_ref[0] = qkv[:, qd + kd:].T.astype(BF16)


def _score_bound(q_gain, k_gain):
    gq = jnp.max(jnp.abs(q_gain), axis=-1, keepdims=True)
    gk = jnp.max(jnp.abs(k_gain), axis=-1, keepdims=True)
    return (1.01 * ATT_HEAD_DIM ** 0.5 * math.log2(math.e)) * gq * gk


def _attn_in(x, g, w, qg, kg, cos, sin, ones, shifted):
    b, T, _ = x.shape
    tm = min(TOKEN_TILE, T)
    qkv_dim = (N_HEADS + 2 * N_KV_HEADS) * ATT_HEAD_DIM
    hw = LANES if shifted else ATT_HEAD_DIM
    fixed = lambda i, t: (0, 0)
    return pl.pallas_call(
        functools.partial(_attn_in_kernel, shifted=shifted),
        grid=(b, T // tm),
        in_specs=[pl.BlockSpec((1, tm, D_MODEL), lambda i, t: (i, t, 0)),
                  pl.BlockSpec((1, D_MODEL), fixed, **_RESIDENT),
                  pl.BlockSpec((D_MODEL, qkv_dim), fixed, **_RESIDENT),
                  pl.BlockSpec((1, LANES), fixed, **_RESIDENT),
                  pl.BlockSpec((1, LANES), fixed, **_RESIDENT),
                  pl.BlockSpec((tm, LANES), lambda i, t: (t, 0)),
                  pl.BlockSpec((tm, LANES), lambda i, t: (t, 0)),
                  pl.BlockSpec((LANES, LANES), fixed, **_RESIDENT)],
        out_specs=[pl.BlockSpec((1, tm, N_HEADS * hw), lambda i, t: (i, t, 0)),
                   pl.BlockSpec((1, N_KV_HEADS, tm, hw), lambda i, t: (i, 0, t, 0)),
                   pl.BlockSpec((1, N_KV_HEADS * ATT_HEAD_DIM, tm), lambda i, t: (i, 0, t))],
        out_shape=[jax.ShapeDtypeStruct((b, T, N_HEADS * hw), BF16),
                   jax.ShapeDtypeStruct((b, N_KV_HEADS, T, hw), BF16),
                   jax.ShapeDtypeStruct((b, N_KV_HEADS * ATT_HEAD_DIM, T), BF16)],
        compiler_params=_params(2),
        name="attn_in",
    )(x, g, w, qg, kg, cos, sin, ones)


def _flash_kernel(q_ref, k_ref, vt_ref, o_ref, s_ref):
    tq = q_ref.shape[1]
    T = k_ref.shape[2]
    tk = s_ref.shape[1]
    dh = ATT_HEAD_DIM
    ones = jnp.ones((2 * SUBLANES, tk), BF16)
    nq = KV_REP * tq
    nk = T // tk

    def scores(g, c, slot):
        q4 = jnp.concatenate([q_ref[0, :, (g * KV_REP + r) * dh:(g * KV_REP + r + 1) * dh]
                              for r in range(KV_REP)], axis=0)
        k0 = pl.multiple_of(c * tk, tk)
        kc = k_ref[0, g, pl.ds(k0, tk), :]
        s_ref[slot] = lax.dot_general(kc, q4, (((1,), (1,)), ((), ())),
                                      preferred_element_type=F32)

    def attend(g, c, slot, m, acc):
        k0 = pl.multiple_of(c * tk, tk)
        va = jnp.concatenate([vt_ref[0, g * dh:(g + 1) * dh, pl.ds(k0, tk)], ones], axis=0)
        s = s_ref[slot]
        m_new = jnp.maximum(m, jnp.max(s, axis=0, keepdims=True))
        p = jnp.exp2(s - m_new).astype(BF16)
        acc = jnp.exp2(m - m_new) * acc + jnp.dot(va, p, preferred_element_type=F32)
        return m_new, acc

    scores(0, 0, 0)
    per_trip = min(ATT_TRIP, nk)
    for g in range(N_KV_HEADS):

        def body(i, carry, g=g):
            m, acc = carry
            c = per_trip * i
            for u in range(per_trip):
                scores(g, c + u + 1, (u + 1) % 2)
                m, acc = attend(g, c + u, u % 2, m, acc)
            return m, acc

        init = (jnp.full((1, nq), -jnp.inf, F32), jnp.zeros((dh + 2 * SUBLANES, nq), F32))
        m, acc = lax.fori_loop(0, nk // per_trip - 1, body, init)
        for u in range(per_trip):
            c = nk - per_trip + u
            if c + 1 < nk:
                scores(g, c + 1, (u + 1) % 2)
            elif g + 1 < N_KV_HEADS:
                scores(g + 1, 0, 0)
            m, acc = attend(g, c, u % 2, m, acc)
        o4 = acc[:dh] / acc[dh:dh + 1]
        ot = jnp.concatenate([o4[:, r * tq:(r + 1) * tq] for r in range(KV_REP)], axis=0)
        o_ref[0, :, g * KV_REP * dh:(g + 1) * KV_REP * dh] = ot.T.astype(BF16)


def _flash(q, k, vt):
    b, T, qd = q.shape
    tq = min(ATT_Q_TILE, T)
    tk = min(ATT_K_TILE, T // 2)
    per_trip = min(ATT_TRIP, T // tk)
    assert T % tk == 0 and per_trip % 2 == 0 and (T // tk) % per_trip == 0, (T, tk, per_trip)
    return pl.pallas_call(
        _flash_kernel,
        grid=(b, T // tq),
        in_specs=[pl.BlockSpec((1, tq, qd), lambda i, t: (i, t, 0)),
                  pl.BlockSpec((1, N_KV_HEADS, T, ATT_HEAD_DIM), lambda i, t: (i, 0, 0, 0)),
                  pl.BlockSpec((1, N_KV_HEADS * ATT_HEAD_DIM, T), lambda i, t: (i, 0, 0))],
        out_specs=pl.BlockSpec((1, tq, qd), lambda i, t: (i, t, 0)),
        out_shape=jax.ShapeDtypeStruct((b, T, qd), BF16),
        scratch_shapes=[pltpu.VMEM((2, tk, KV_REP * tq), F32)],
        compiler_params=_params(2),
        name="flash",
    )(q, k, vt)


def _flash_shifted_kernel(q_ref, k_ref, vt_ref, o_ref, p_ref):
    tq = q_ref.shape[1]
    T = k_ref.shape[2]
    tk = p_ref.shape[1]
    dh = ATT_HEAD_DIM
    ones = jnp.ones((2 * SUBLANES, tk), BF16)
    nq = KV_REP * tq
    nk = T // tk

    def probs(g, c, slot):
        q4 = jnp.concatenate([q_ref[0, :, (g * KV_REP + r) * LANES:(g * KV_REP + r + 1) * LANES]
                              for r in range(KV_REP)], axis=0)
        k0 = pl.multiple_of(c * tk, tk)
        s = lax.dot_general(k_ref[0, g, pl.ds(k0, tk), :], q4, (((1,), (1,)), ((), ())),
                            preferred_element_type=F32)
        p_ref[slot] = jnp.exp2(s).astype(BF16)

    def accumulate(g, c, slot, acc):
        k0 = pl.multiple_of(c * tk, tk)
        va = jnp.concatenate([vt_ref[0, g * dh:(g + 1) * dh, pl.ds(k0, tk)], ones], axis=0)
        return acc + jnp.dot(va, p_ref[slot], preferred_element_type=F32)

    probs(0, 0, 0)
    per_trip = min(ATT_TRIP, nk)
    for g in range(N_KV_HEADS):

        def body(i, acc, g=g):
            c = per_trip * i
            for u in range(per_trip):
                probs(g, c + u + 1, (u + 1) % 2)
                acc = accumulate(g, c + u, u % 2, acc)
            return acc

        acc = lax.fori_loop(0, nk // per_trip - 1, body, jnp.zeros((dh + 2 * SUBLANES, nq), F32))
        for u in range(per_trip):
            c = nk - per_trip + u
            if c + 1 < nk:
                probs(g, c + 1, (u + 1) % 2)
            elif g + 1 < N_KV_HEADS:
                probs(g + 1, 0, 0)
            acc = accumulate(g, c, u % 2, acc)
        o4 = acc[:dh] / acc[dh:dh + 1]
        ot = jnp.concatenate([o4[:, r * tq:(r + 1) * tq] for r in range(KV_REP)], axis=0)
        o_ref[0, :, g * KV_REP * dh:(g + 1) * KV_REP * dh] = ot.T.astype(BF16)


def _flash_shifted(q, k, vt):
    b, T, _ = q.shape
    qd = N_HEADS * ATT_HEAD_DIM
    tq = min(ATT_Q_TILE, T)
    tk = min(ATT_K_TILE, T // 2)
    per_trip = min(ATT_TRIP, T // tk)
    assert T % tk == 0 and per_trip % 2 == 0 and (T // tk) % per_trip == 0, (T, tk, per_trip)
    return pl.pallas_call(
        _flash_shifted_kernel,
        grid=(b, T // tq),
        in_specs=[pl.BlockSpec((1, tq, N_HEADS * LANES), lambda i, t: (i, t, 0)),
                  pl.BlockSpec((1, N_KV_HEADS, T, LANES), lambda i, t: (i, 0, 0, 0)),
                  pl.BlockSpec((1, N_KV_HEADS * ATT_HEAD_DIM, T), lambda i, t: (i, 0, 0))],
        out_specs=pl.BlockSpec((1, tq, qd), lambda i, t: (i, t, 0)),
        out_shape=jax.ShapeDtypeStruct((b, T, qd), BF16),
        scratch_shapes=[pltpu.VMEM((2, tk, KV_REP * tq), BF16)],
        compiler_params=_params(2),
        name="flash_shifted",
    )(q, k, vt)


def _rope_tables(T):
    t = jnp.arange(T, dtype=jnp.int32)
    pos = jnp.stack([(t // GRID_W).astype(F32), (t % GRID_W).astype(F32)], axis=1)
    inv_freq = ROPE_THETA ** (-jnp.arange(0, AXIS_DIM, 2, dtype=F32) / AXIS_DIM)
    ang = pos[:, :, None] * inv_freq[None, None, :]
    cos = jnp.cos(ang)
    sin = jnp.sin(ang)
    cos_d = jnp.concatenate([cos, cos], axis=-1).reshape(T, ATT_HEAD_DIM)
    sin_d = jnp.concatenate([-sin, sin], axis=-1).reshape(T, ATT_HEAD_DIM)
    return jnp.tile(cos_d, (1, LANES // ATT_HEAD_DIM)), jnp.tile(sin_d, (1, LANES // ATT_HEAD_DIM))


def _group_order(v):
    return v.reshape(2, SSM_GROUPS, HEADS_PER_GROUP).transpose(1, 0, 2).reshape(SSM_GROUPS, 2 * HEADS_PER_GROUP)


def kernel(x_prompt, x_sample, norm_g, ffn_w_gate, ffn_w_up, ffn_w_down, ssm_w_in, ssm_conv_w,
           ssm_conv_b, ssm_dt_bias, ssm_A_log, ssm_D, ssm_norm_g, ssm_w_out, attn_w_qkv,
           attn_q_norm, attn_k_norm, attn_w_out, final_norm):
    nb_prompt, T, _ = x_prompt.shape
    b = nb_prompt + x_sample.shape[0]
    n = b * T
    x = (x_prompt.reshape(nb_prompt * T, D_MODEL), x_sample.reshape(n - nb_prompt * T, D_MODEL))
    depth = norm_g.shape[0]

    wg = ffn_w_gate.astype(BF16)
    wu = ffn_w_up.astype(BF16)
    wd = ffn_w_down.astype(BF16)
    fg = final_norm.reshape(1, D_MODEL)
    cos, sin = _rope_tables(T)
    half = jnp.arange(LANES) // ATT_HEAD_DIM
    head_ones = (half[:, None] == half[None, :]).astype(BF16)

    for i in range(depth):
        j = i // 2
        x = _ffn(x, norm_g[i, 0].reshape(1, D_MODEL), wg[i, 0], wu[i, 0], wd[i, 0], fg, False,
                 nb_prompt * T)
        g_mix = norm_g[i, 1].reshape(1, D_MODEL)
        if i % 2 == 0:
            w_in = ssm_w_in[j]
            wz = w_in[:, :D_INNER].astype(BF16)
            wx = w_in[:, D_INNER:D_INNER + CONV_DIM].astype(BF16)
            wdt = w_in[:, D_INNER + CONV_DIM:].reshape(D_MODEL, 2, SSM_GROUPS, HEADS_PER_GROUP)
            wdt = wdt.transpose(0, 2, 1, 3).reshape(D_MODEL, 2 * SSM_HEADS)
            wdt = jnp.pad(wdt, ((0, 0), (0, LANES - 2 * SSM_HEADS))).astype(BF16)
            z, xbc, dt_raw = _ssd_in(x.reshape(b, T, D_MODEL), g_mix, wz, wx, wdt,
                                     ssm_conv_w[j], ssm_conv_b[j].reshape(1, CONV_DIM))
            dt_rows = dt_raw[:, :, :2 * SSM_HEADS].reshape(b, T, SSM_GROUPS, 2 * HEADS_PER_GROUP)
            dt_rows = dt_rows.transpose(0, 2, 3, 1)
            y = _ssd(xbc, z, dt_rows,
                     _group_order(ssm_dt_bias[j])[:, :, None],
                     _group_order(ssm_A_log[j])[:, :, None],
                     jnp.repeat(ssm_D[j], SSM_HEAD_DIM).reshape(SSM_GROUPS, GROUP_DIM, 1),
                     ssm_norm_g[j].reshape(SSM_GROUPS, 1, GROUP_DIM))
            mix = (y.reshape(n, D_INNER), ssm_w_out[j].astype(BF16))
        else:
            qg = jnp.tile(attn_q_norm[j], LANES // ATT_HEAD_DIM).reshape(1, LANES)
            kg = jnp.tile(attn_k_norm[j], LANES // ATT_HEAD_DIM).reshape(1, LANES)
            w_qkv = attn_w_qkv[j].astype(BF16)

            def attend(x3, shifted, qg=qg, kg=kg, w_qkv=w_qkv, g_mix=g_mix):
                q, k, vt = _attn_in(x3, g_mix, w_qkv, qg, kg, cos, sin, head_ones, shifted)
                return _flash_shifted(q, k, vt) if shifted else _flash(q, k, vt)

            o = lax.cond(2.0 * _score_bound(qg, kg)[0, 0] < SHIFT_SAFE_LOG2,
                         functools.partial(attend, shifted=True),
                         functools.partial(attend, shifted=False),
                         x.reshape(b, T, D_MODEL))
            mix = (o.reshape(n, N_HEADS * ATT_HEAD_DIM), attn_w_out[j].astype(BF16))
        last = i == depth - 1
        x = _ffn(x, norm_g[i, 2].reshape(1, D_MODEL), wg[i, 1], wu[i, 1], wd[i, 1], fg,
                 last, nb_prompt * T, mix, split_out=last)
    y_prompt, y_sample = x
    return (y_prompt.reshape(x_prompt.shape), y_sample.reshape(x_sample.shape))
```

```python
import functools
import math

import jax
import jax.numpy as jnp
from jax import lax
from jax.experimental import pallas as pl
from jax.experimental.pallas import tpu as pltpu

F32 = jnp.float32
BF16 = jnp.bfloat16

D_MODEL = 1024
GRID_W = 64
D_FF = 2816
D_INNER = 2048
SSM_HEAD_DIM = 64
SSM_HEADS = 32
SSM_GROUPS = 8
HEADS_PER_GROUP = 4
GROUP_DIM = HEADS_PER_GROUP * SSM_HEAD_DIM
D_STATE = 128
D_CONV = 5
CONV_DIM = D_INNER + 2 * SSM_GROUPS * D_STATE
ATT_HEAD_DIM = 64
N_HEADS = 16
N_KV_HEADS = 4
KV_REP = 4
AXIS_DIM = 32
ROPE_THETA = 10000.0
EPS = 1e-6

LANES = 128
SUBLANES = 8
VMEM_LIMIT = 56 * 1024 * 1024

TOKEN_TILE = 512
SSD_CHUNK = 128
SSD_TRIP = 8
RELAYOUT_ROWS = 256
IN_SLAB = 256
CONV_ROWS = 64
CONV_HALO = 8
ATT_Q_TILE = 256
ATT_Q_TILE_SHIFTED = 512
ATT_K_TILE = 512
ATT_TRIP = 4
SHIFT_SAFE_LOG2 = 100.0
FF_SPLITS = (1024, 1024, 768)

_RESIDENT = dict(pipeline_mode=pl.Buffered(1))


def _params(n_axes):
    return pltpu.CompilerParams(dimension_semantics=("parallel",) * n_axes,
                                vmem_limit_bytes=VMEM_LIMIT)


def _sigmoid(x):
    return 1.0 / (1.0 + jnp.exp(-x))


def _rms(x, g):
    return x * lax.rsqrt(jnp.mean(x * x, axis=-1, keepdims=True) + EPS) * g


def _ffn_kernel(*refs, final, mixed, n_x, n_out, split):
    refs = list(refs)
    x_refs, refs = refs[:n_x], refs[n_x:]
    step = pl.program_id(0)
    if n_x == 2:
        x = jnp.where(step < split, x_refs[0][...], x_refs[1][...])
    else:
        x = x_refs[0][...]
    if mixed:
        (y_ref, wo_ref), refs = refs[:2], refs[2:]
        x = x + jnp.dot(y_ref[...], wo_ref[...], preferred_element_type=F32)
    g_ref, wg_ref, wu_ref, wd_ref, fg_ref = refs[:5]
    o_refs = refs[5:]
    h = _rms(x, g_ref[...]).astype(BF16)
    acc = None
    c0 = 0
    for cw in FF_SPLITS:
        gate = jnp.dot(h, wg_ref[:, c0:c0 + cw], preferred_element_type=F32)
        up = jnp.dot(h, wu_ref[:, c0:c0 + cw], preferred_element_type=F32)
        a = (gate * _sigmoid(gate) * up).astype(BF16)
        part = jnp.dot(a, wd_ref[c0:c0 + cw, :], preferred_element_type=F32)
        acc = part if acc is None else acc + part
        c0 += cw
    y = x + 0.5 * acc
    if final:
        y = _rms(y, fg_ref[...])
    if n_out == 2:
        @pl.when(step < split)
        def _():
            o_refs[0][...] = y

        @pl.when(step >= split)
        def _():
            o_refs[1][...] = y
    else:
        o_refs[0][...] = y


def _ffn(xs, g, wg, wu, wd, fg, final, n_prompt, mix=None, split_out=False):
    xs = xs if isinstance(xs, tuple) else (xs,)
    n = sum(x.shape[0] for x in xs)
    tm = min(TOKEN_TILE, n_prompt)
    split = n_prompt // tm
    row = lambda i: (i, 0)
    first = lambda i: (jnp.minimum(i, split - 1), 0)
    second = lambda i: (jnp.maximum(i - split, 0), 0)
    fixed = lambda i: (0, 0)
    args = list(xs)
    tile = lambda m: pl.BlockSpec((tm, D_MODEL), m)
    in_specs = [tile(first), tile(second)] if len(xs) == 2 else [tile(row)]
    if mix is not None:
        y, wo = mix
        k = y.shape[1]
        args += [y, wo]
        in_specs += [pl.BlockSpec((tm, k), row), pl.BlockSpec((k, D_MODEL), fixed, **_RESIDENT)]
    args += [g, wg, wu, wd, fg]
    in_specs += [pl.BlockSpec((1, D_MODEL), fixed, **_RESIDENT),
                 pl.BlockSpec((D_MODEL, D_FF), fixed, **_RESIDENT),
                 pl.BlockSpec((D_MODEL, D_FF), fixed, **_RESIDENT),
                 pl.BlockSpec((D_FF, D_MODEL), fixed, **_RESIDENT),
                 pl.BlockSpec((1, D_MODEL), fixed, **_RESIDENT)]
    if split_out:
        n_split = split * tm
        out_specs = [tile(first), tile(second)]
        out_shape = [jax.ShapeDtypeStruct((n_split, D_MODEL), F32),
                     jax.ShapeDtypeStruct((n - n_split, D_MODEL), F32)]
    else:
        out_specs = tile(row)
        out_shape = jax.ShapeDtypeStruct((n, D_MODEL), F32)
    return pl.pallas_call(
        functools.partial(_ffn_kernel, final=final, mixed=mix is not None, n_x=len(xs),
                          n_out=2 if split_out else 1, split=split),
        grid=(n // tm,),
        in_specs=in_specs,
        out_specs=out_specs,
        out_shape=out_shape,
        compiler_params=pltpu.CompilerParams(dimension_semantics=("arbitrary",),
                                             vmem_limit_bytes=VMEM_LIMIT),
        name="ffn",
    )(*args)


def _ssd_in_kernel(x_ref, xp_ref, xn_ref, g_ref, wz_ref, wx_ref, wdt_ref, cw_ref, cb_ref,
                   z_ref, xbc_ref, dt_ref, pre_ref):
    t = pl.program_id(1)
    tm = x_ref.shape[1]
    H = CONV_HALO
    g = g_ref[...]
    hm = _rms(x_ref[0], g)
    h = hm.astype(BF16)
    hp = jnp.where(t > 0, _rms(xp_ref[0], g), 0.0)
    hn = jnp.where(t < pl.num_programs(1) - 1, _rms(xn_ref[0], g), 0.0)
    h_ext = jnp.concatenate([hp, hm, hn], axis=0).astype(BF16)
    row0 = jnp.minimum(t, 0)

    def project(s):
        c0 = s * IN_SLAB
        res = jnp.dot(h_ext, wx_ref[:, c0:c0 + IN_SLAB], preferred_element_type=F32)
        for j in range(IN_SLAB // LANES):
            pre_ref[c0 // LANES + j] = res[:, j * LANES:(j + 1) * LANES]

    def conv(s):
        for c0 in range(s * IN_SLAB, (s + 1) * IN_SLAB, LANES):
            for q0 in range(0, tm, CONV_ROWS):
                y = jnp.broadcast_to(cb_ref[:, c0:c0 + LANES], (CONV_ROWS, LANES))
                for k in range(D_CONV):
                    r0 = row0 + (q0 + H - D_CONV // 2 + k)
                    y = y + (cw_ref[k:k + 1, c0:c0 + LANES]
                             * pre_ref[c0 // LANES, pl.ds(r0, CONV_ROWS), :])
                xbc_ref[0, q0:q0 + CONV_ROWS, c0:c0 + LANES] = (y * _sigmoid(y)).astype(BF16)

    n_slab = CONV_DIM // IN_SLAB
    project(0)
    for s in range(1, n_slab):
        project(s)
        conv(s - 1)
    for c0 in range(0, D_INNER, IN_SLAB):
        z_ref[0, :, c0:c0 + IN_SLAB] = jnp.dot(h, wz_ref[:, c0:c0 + IN_SLAB],
                                               preferred_element_type=F32).astype(BF16)
        if c0 == 0:
            conv(n_slab - 1)
    dt_ref[0] = jnp.dot(h, wdt_ref[...], preferred_element_type=F32)


def _ssd_in(x, g, wz, wx, wdt, conv_w, conv_b):
    b, T, _ = x.shape
    tm = min(TOKEN_TILE, T)
    hb = tm // CONV_HALO
    fixed = lambda i, t: (0, 0)
    tile = lambda i, t: (i, t, 0)
    return pl.pallas_call(
        _ssd_in_kernel,
        grid=(b, T // tm),
        in_specs=[pl.BlockSpec((1, tm, D_MODEL), tile),
                  pl.BlockSpec((1, CONV_HALO, D_MODEL),
                               lambda i, t: (i, jnp.maximum(t * hb - 1, 0), 0)),
                  pl.BlockSpec((1, CONV_HALO, D_MODEL),
                               lambda i, t: (i, jnp.minimum((t + 1) * hb, T // CONV_HALO - 1), 0)),
                  pl.BlockSpec((1, D_MODEL), fixed, **_RESIDENT),
                  pl.BlockSpec((D_MODEL, D_INNER), fixed, **_RESIDENT),
                  pl.BlockSpec((D_MODEL, CONV_DIM), fixed, **_RESIDENT),
                  pl.BlockSpec((D_MODEL, LANES), fixed, **_RESIDENT),
                  pl.BlockSpec((D_CONV, CONV_DIM), fixed, **_RESIDENT),
                  pl.BlockSpec((1, CONV_DIM), fixed, **_RESIDENT)],
        out_specs=[pl.BlockSpec((1, tm, D_INNER), tile),
                   pl.BlockSpec((1, tm, CONV_DIM), tile),
                   pl.BlockSpec((1, tm, LANES), tile)],
        out_shape=[jax.ShapeDtypeStruct((b, T, D_INNER), BF16),
                   jax.ShapeDtypeStruct((b, T, CONV_DIM), BF16),
                   jax.ShapeDtypeStruct((b, T, LANES), F32)],
        scratch_shapes=[pltpu.VMEM((CONV_DIM // LANES, tm + 2 * CONV_HALO, LANES), F32)],
        compiler_params=_params(2),
        name="ssd_in",
    )(x, x, x, g, wz, wx, wdt, conv_w, conv_b)


_R_CS, _R_DT, _R_ECS, _R_WEND, _R_ETOT, _R_SPLIT = 0, 8, 16, 24, 32, 40
_N_ROWQ = 64
_NEG = -1e30


def _ssd_kernel(x_ref, b_ref, c_ref, z_ref, dt_ref,
                dtb_ref, alog_ref, d_ref, ng_ref,
                o_ref,
                xt_ref, ct_ref, rowq_ref, cscol_ref, sb_ref, fr_ref, ysb_ref, upd_ref):
    T = x_ref.shape[1]
    Q = SSD_CHUNK
    nc = T // Q
    assert nc % SSD_TRIP == 0 and SSD_TRIP % 2 == 0, (T, SSD_TRIP)
    R = min(RELAYOUT_ROWS, T)
    hp = HEADS_PER_GROUP
    P = SSM_HEAD_DIM

    def relayout(i, carry):
        r0 = pl.multiple_of(i * R, R)
        for s in range(GROUP_DIM // LANES):
            xt_ref[s * LANES:(s + 1) * LANES, pl.ds(r0, R)] = (
                x_ref[0, pl.ds(r0, R), s * LANES:(s + 1) * LANES].astype(F32).T)
        ct_ref[:, pl.ds(r0, R)] = c_ref[0, pl.ds(r0, R), :].astype(F32).T.astype(BF16)
        return carry

    lax.fori_loop(0, T // R, relayout, 0, unroll=2)

    raw = dt_ref[0, 0] + dtb_ref[0]
    dt = jnp.maximum(raw, 0.0) + jnp.log1p(jnp.exp(-jnp.abs(raw)))
    a = dt * (-jnp.exp(alog_ref[0]) * math.log2(math.e))
    lane = lax.broadcasted_iota(jnp.int32, (2 * hp, T), 1) % Q
    is_fwd = lax.broadcasted_iota(jnp.int32, (2 * hp, T), 0) < hp
    pre, suf = a, a
    s = 1
    while s < Q:
        pre = pre + jnp.where(lane >= s, pltpu.roll(pre, s, 1), 0.0)
        suf = suf + jnp.where(lane < Q - s, pltpu.roll(suf, T - s, 1), 0.0)
        s *= 2
    cs = jnp.where(is_fwd, pre, suf)
    rowq_ref[_R_CS:_R_CS + 8, :] = cs
    rowq_ref[_R_DT:_R_DT + 8, :] = dt
    rowq_ref[_R_ECS:_R_ECS + 8, :] = jnp.exp2(cs)
    rowq_ref[_R_WEND:_R_WEND + 8, :] = jnp.exp2(jnp.where(is_fwd, suf, pre) - a) * dt
    rowq_ref[_R_ETOT:_R_ETOT + 8, :] = jnp.exp2(pre + suf - a)
    hi = cs.astype(BF16).astype(F32)
    mid = (cs - hi).astype(BF16).astype(F32)
    rowq_ref[_R_SPLIT:_R_SPLIT + 8, :] = hi
    rowq_ref[_R_SPLIT + 8:_R_SPLIT + 16, :] = mid
    rowq_ref[_R_SPLIT + 16:_R_SPLIT + 24, :] = cs - hi - mid

    def to_cols(c, carry):
        t0 = pl.multiple_of(c * Q, Q)
        tile = jnp.concatenate([rowq_ref[_R_SPLIT:_R_SPLIT + 24, pl.ds(t0, Q)],
                                jnp.zeros((LANES - 24, Q), F32)], axis=0)
        cscol_ref[pl.ds(t0, Q), :] = tile.T.astype(BF16)
        return carry

    lax.fori_loop(0, nc, to_cols, 0, unroll=8)

    def by_head(mat, rows, off):
        return jnp.concatenate([mat[r * P:(r + 1) * P] * rows[off + r:off + r + 1, :]
                                for r in range(hp)], axis=0)

    def state_update(t0, direction):
        xw = by_head(xt_ref[:, pl.ds(t0, Q)], rowq_ref[_R_WEND:_R_WEND + 8, pl.ds(t0, Q)],
                     direction * hp)
        return jnp.dot(xw.astype(BF16), b_ref[0, pl.ds(t0, Q), :], preferred_element_type=F32)

    def state_decay(S, t0, direction):
        return by_head(S, rowq_ref[_R_ETOT:_R_ETOT + 8, pl.ds(t0, Q)], direction * hp)

    def bwd_body(k, S):
        c = nc - 1 - k
        sb_ref[c] = S.astype(BF16)
        t0 = pl.multiple_of(c * Q, Q)
        return state_decay(S, t0, 1) + state_update(t0, 1)

    lax.fori_loop(0, nc, bwd_body, jnp.zeros((GROUP_DIM, D_STATE), F32), unroll=4)

    src = lax.broadcasted_iota(jnp.int32, (Q, Q), 0)
    dst = lax.broadcasted_iota(jnp.int32, (Q, Q), 1)
    causal = dst >= src
    anti = dst <= src
    dmat = jnp.broadcast_to(d_ref[0], (GROUP_DIM, Q))
    sel_k = lax.broadcasted_iota(jnp.int32, (LANES, 2 * hp * Q), 0)
    sel_q = lax.broadcasted_iota(jnp.int32, (LANES, 2 * hp * Q), 1) // Q
    sel = jnp.where((sel_k < 24) & (sel_k % 8 == sel_q), 1.0, 0.0).astype(BF16)
    ng = ng_ref[0]

    def front(c, slot):
        t0 = pl.multiple_of(c * Q, Q)
        ct = ct_ref[:, pl.ds(t0, Q)]
        fr_ref[slot, :, 0:Q] = jnp.dot(b_ref[0, pl.ds(t0, Q), :], ct,
                                       preferred_element_type=F32)
        fr_ref[slot, :, Q:] = jnp.dot(cscol_ref[pl.ds(t0, Q), :], sel,
                                      preferred_element_type=F32)
        ysb_ref[slot] = jnp.dot(sb_ref[c], ct, preferred_element_type=F32)
        upd_ref[slot] = state_update(t0, 0)

    def back(c, slot, S):
        t0 = pl.multiple_of(c * Q, Q)
        ct = ct_ref[:, pl.ds(t0, Q)]
        xt = xt_ref[:, pl.ds(t0, Q)]
        csr = rowq_ref[_R_CS:_R_CS + 8, pl.ds(t0, Q)]
        dtr = rowq_ref[_R_DT:_R_DT + 8, pl.ds(t0, Q)]
        ecs = rowq_ref[_R_ECS:_R_ECS + 8, pl.ds(t0, Q)]
        cbt = fr_ref[slot, :, 0:Q]
        y_sf = jnp.dot(S.astype(BF16), ct, preferred_element_type=F32)
        ys = []
        for r in range(hp):
            cf = fr_ref[slot, :, (1 + r) * Q:(2 + r) * Q]
            cb = fr_ref[slot, :, (1 + hp + r) * Q:(2 + hp + r) * Q]
            mf = cbt * jnp.exp2(jnp.where(causal, csr[r:r + 1, :] - cf, _NEG))
            mb = cbt * jnp.exp2(jnp.where(anti, csr[hp + r:hp + r + 1, :] - cb, _NEG))
            xr = xt[r * P:(r + 1) * P]
            lhs = jnp.concatenate([xr * dtr[r:r + 1, :], xr * dtr[hp + r:hp + r + 1, :]], axis=1)
            rhs = jnp.concatenate([mf, mb], axis=0)
            ys.append(jnp.dot(lhs.astype(BF16), rhs.astype(BF16), preferred_element_type=F32))
        yt = jnp.concatenate(ys, axis=0)
        yt = yt + by_head(y_sf, ecs, 0) + by_head(ysb_ref[slot], ecs, hp) + dmat * xt
        y = yt.T
        zc = z_ref[0, pl.ds(t0, Q), :].astype(F32)
        y = y * (zc * _sigmoid(zc))
        o_ref[0, pl.ds(t0, Q), :] = _rms(y, ng).astype(BF16)
        return state_decay(S, t0, 0) + upd_ref[slot]

    def trip(i, S):
        c = SSD_TRIP * i
        for u in range(SSD_TRIP):
            front(jnp.minimum(c + u + 1, nc - 1), (u + 1) % 2)
            S = back(c + u, u % 2, S)
        return S

    front(0, 0)
    lax.fori_loop(0, nc // SSD_TRIP, trip, jnp.zeros((GROUP_DIM, D_STATE), F32))


def _ssd(xbc, z, dt_rows, dt_bias, a_log, d_vec, norm_g):
    b, T, _ = xbc.shape
    nb = D_INNER // D_STATE
    xblk = lambda i, g: (i, 0, g)
    per_g = lambda i, g: (g, 0, 0)
    return pl.pallas_call(
        _ssd_kernel,
        grid=(b, SSM_GROUPS),
        in_specs=[pl.BlockSpec((1, T, GROUP_DIM), xblk),
                  pl.BlockSpec((1, T, D_STATE), lambda i, g: (i, 0, nb + g)),
                  pl.BlockSpec((1, T, D_STATE), lambda i, g: (i, 0, nb + SSM_GROUPS + g)),
                  pl.BlockSpec((1, T, GROUP_DIM), xblk),
                  pl.BlockSpec((1, 1, 2 * HEADS_PER_GROUP, T), lambda i, g: (i, g, 0, 0)),
                  pl.BlockSpec((1, 2 * HEADS_PER_GROUP, 1), per_g),
                  pl.BlockSpec((1, 2 * HEADS_PER_GROUP, 1), per_g),
                  pl.BlockSpec((1, GROUP_DIM, 1), per_g),
                  pl.BlockSpec((1, 1, GROUP_DIM), per_g)],
        out_specs=pl.BlockSpec((1, T, GROUP_DIM), xblk),
        out_shape=jax.ShapeDtypeStruct((b, T, D_INNER), BF16),
        scratch_shapes=[pltpu.VMEM((GROUP_DIM, T), F32),
                        pltpu.VMEM((D_STATE, T), BF16),
                        pltpu.VMEM((_N_ROWQ, T), F32),
                        pltpu.VMEM((T, LANES), BF16),
                        pltpu.VMEM((T // SSD_CHUNK, GROUP_DIM, D_STATE), BF16),
                        pltpu.VMEM((2, SSD_CHUNK, (1 + 2 * HEADS_PER_GROUP) * SSD_CHUNK), F32),
                        pltpu.VMEM((2, GROUP_DIM, SSD_CHUNK), F32),
                        pltpu.VMEM((2, GROUP_DIM, D_STATE), F32)],
        compiler_params=_params(2),
        name="ssd",
    )(xbc, xbc, xbc, z, dt_rows, dt_bias, a_log, d_vec, norm_g)


def _attn_in_kernel(x_ref, g_ref, w_ref, qg_ref, kg_ref, cos_ref, sin_ref, ones_ref,
                    q_ref, k_ref, vt_ref, *, shifted):
    h = _rms(x_ref[0], g_ref[...]).astype(BF16)
    qkv = jnp.dot(h, w_ref[...], preferred_element_type=F32)
    cos = cos_ref[...]
    sin = sin_ref[...]
    ones = ones_ref[...]
    tm = h.shape[0]
    first_half = (lax.broadcasted_iota(jnp.int32, (tm, LANES), 1) % AXIS_DIM) < AXIS_DIM // 2
    qd = N_HEADS * ATT_HEAD_DIM
    kd = N_KV_HEADS * ATT_HEAD_DIM

    def norm_rope(v, gain):
        sq = v * v
        hi = sq.astype(BF16)
        lo = (sq - hi.astype(F32)).astype(BF16)
        ss = (jnp.dot(hi, ones, preferred_element_type=F32)
              + jnp.dot(lo, ones, preferred_element_type=F32))
        n = v * lax.rsqrt(ss * (1.0 / ATT_HEAD_DIM) + EPS) * gain
        partner = jnp.where(first_half,
                            pltpu.roll(n, LANES - AXIS_DIM // 2, 1),
                            pltpu.roll(n, AXIS_DIM // 2, 1))
        return n * cos + partner * sin

    qgain = qg_ref[...] * (ATT_HEAD_DIM ** -0.5 * math.log2(math.e))
    lane = lax.broadcasted_iota(jnp.int32, (tm, LANES), 1)
    bound = _score_bound(qg_ref[...], kg_ref[...])

    def padded(v, extra):
        tail = jnp.where(lane == ATT_HEAD_DIM, extra, 0.0)
        return (jnp.where(lane < ATT_HEAD_DIM, v, tail).astype(BF16),
                jnp.where(lane < ATT_HEAD_DIM, pltpu.roll(v, ATT_HEAD_DIM, 1), tail).astype(BF16))

    for s in range(qd // LANES):
        qq = norm_rope(qkv[:, s * LANES:(s + 1) * LANES], qgain)
        if shifted:
            qa, qb = padded(qq, 1.0)
            q_ref[0, :, 2 * s * LANES:(2 * s + 1) * LANES] = qa
            q_ref[0, :, (2 * s + 1) * LANES:(2 * s + 2) * LANES] = qb
        else:
            q_ref[0, :, s * LANES:(s + 1) * LANES] = qq.astype(BF16)
    for s in range(kd // LANES):
        kk = norm_rope(qkv[:, qd + s * LANES:qd + (s + 1) * LANES], kg_ref[...])
        if shifted:
            k_ref[0, 2 * s], k_ref[0, 2 * s + 1] = padded(kk, -bound)
        else:
            kk = kk.astype(BF16)
            k_ref[0, 2 * s] = kk[:, :ATT_HEAD_DIM]
            k_ref[0, 2 * s + 1] = kk[:, ATT_HEAD_DIM:]
    vt_ref[0] = qkv[:, qd + kd:].T.astype(BF16)


def _score_bound(q_gain, k_gain):
    gq = jnp.max(jnp.abs(q_gain), axis=-1, keepdims=True)
    gk = jnp.max(jnp.abs(k_gain), axis=-1, keepdims=True)
    return (1.01 * ATT_HEAD_DIM ** 0.5 * math.log2(math.e)) * gq * gk


def _attn_in(x, g, w, qg, kg, cos, sin, ones, shifted):
    b, T, _ = x.shape
    tm = min(TOKEN_TILE, T)
    qkv_dim = (N_HEADS + 2 * N_KV_HEADS) * ATT_HEAD_DIM
    hw = LANES if shifted else ATT_HEAD_DIM
    fixed = lambda i, t: (0, 0)
    return pl.pallas_call(
        functools.partial(_attn_in_kernel, shifted=shifted),
        grid=(b, T // tm),
        in_specs=[pl.BlockSpec((1, tm, D_MODEL), lambda i, t: (i, t, 0)),
                  pl.BlockSpec((1, D_MODEL), fixed, **_RESIDENT),
                  pl.BlockSpec((D_MODEL, qkv_dim), fixed, **_RESIDENT),
                  pl.BlockSpec((1, LANES), fixed, **_RESIDENT),
                  pl.BlockSpec((1, LANES), fixed, **_RESIDENT),
                  pl.BlockSpec((tm, LANES), lambda i, t: (t, 0)),
                  pl.BlockSpec((tm, LANES), lambda i, t: (t, 0)),
                  pl.BlockSpec((LANES, LANES), fixed, **_RESIDENT)],
        out_specs=[pl.BlockSpec((1, tm, N_HEADS * hw), lambda i, t: (i, t, 0)),
                   pl.BlockSpec((1, N_KV_HEADS, tm, hw), lambda i, t: (i, 0, t, 0)),
                   pl.BlockSpec((1, N_KV_HEADS * ATT_HEAD_DIM, tm), lambda i, t: (i, 0, t))],
        out_shape=[jax.ShapeDtypeStruct((b, T, N_HEADS * hw), BF16),
                   jax.ShapeDtypeStruct((b, N_KV_HEADS, T, hw), BF16),
                   jax.ShapeDtypeStruct((b, N_KV_HEADS * ATT_HEAD_DIM, T), BF16)],
        compiler_params=_params(2),
        name="attn_in",
    )(x, g, w, qg, kg, cos, sin, ones)


def _flash_kernel(q_ref, k_ref, vt_ref, o_ref, s_ref):
    tq = q_ref.shape[1]
    T = k_ref.shape[2]
    tk = s_ref.shape[1]
    dh = ATT_HEAD_DIM
    ones = jnp.ones((2 * SUBLANES, tk), BF16)
    nq = KV_REP * tq
    nk = T // tk

    def scores(g, c, slot):
        q4 = jnp.concatenate([q_ref[0, :, (g * KV_REP + r) * dh:(g * KV_REP + r + 1) * dh]
                              for r in range(KV_REP)], axis=0)
        k0 = pl.multiple_of(c * tk, tk)
        kc = k_ref[0, g, pl.ds(k0, tk), :]
        s_ref[slot] = lax.dot_general(kc, q4, (((1,), (1,)), ((), ())),
                                      preferred_element_type=F32)

    def attend(g, c, slot, m, acc):
        k0 = pl.multiple_of(c * tk, tk)
        va = jnp.concatenate([vt_ref[0, g * dh:(g + 1) * dh, pl.ds(k0, tk)], ones], axis=0)
        s = s_ref[slot]
        m_new = jnp.maximum(m, jnp.max(s, axis=0, keepdims=True))
        p = jnp.exp2(s - m_new).astype(BF16)
        acc = jnp.exp2(m - m_new) * acc + jnp.dot(va, p, preferred_element_type=F32)
        return m_new, acc

    scores(0, 0, 0)
    per_trip = min(ATT_TRIP, nk)
    for g in range(N_KV_HEADS):

        def body(i, carry, g=g):
            m, acc = carry
            c = per_trip * i
            for u in range(per_trip):
                scores(g, c + u + 1, (u + 1) % 2)
                m, acc = attend(g, c + u, u % 2, m, acc)
            return m, acc

        init = (jnp.full((1, nq), -jnp.inf, F32), jnp.zeros((dh + 2 * SUBLANES, nq), F32))
        m, acc = lax.fori_loop(0, nk // per_trip - 1, body, init)
        for u in range(per_trip):
            c = nk - per_trip + u
            if c + 1 < nk:
                scores(g, c + 1, (u + 1) % 2)
            elif g + 1 < N_KV_HEADS:
                scores(g + 1, 0, 0)
            m, acc = attend(g, c, u % 2, m, acc)
        o4 = acc[:dh] / acc[dh:dh + 1]
        ot = jnp.concatenate([o4[:, r * tq:(r + 1) * tq] for r in range(KV_REP)], axis=0)
        o_ref[0, :, g * KV_REP * dh:(g + 1) * KV_REP * dh] = ot.T.astype(BF16)


def _flash(q, k, vt):
    b, T, qd = q.shape
    tq = min(ATT_Q_TILE, T)
    tk = min(ATT_K_TILE, T // 2)
    per_trip = min(ATT_TRIP, T // tk)
    assert T % tk == 0 and per_trip % 2 == 0 and (T // tk) % per_trip == 0, (T, tk, per_trip)
    return pl.pallas_call(
        _flash_kernel,
        grid=(b, T // tq),
        in_specs=[pl.BlockSpec((1, tq, qd), lambda i, t: (i, t, 0)),
                  pl.BlockSpec((1, N_KV_HEADS, T, ATT_HEAD_DIM), lambda i, t: (i, 0, 0, 0)),
                  pl.BlockSpec((1, N_KV_HEADS * ATT_HEAD_DIM, T), lambda i, t: (i, 0, 0))],
        out_specs=pl.BlockSpec((1, tq, qd), lambda i, t: (i, t, 0)),
        out_shape=jax.ShapeDtypeStruct((b, T, qd), BF16),
        scratch_shapes=[pltpu.VMEM((2, tk, KV_REP * tq), F32)],
        compiler_params=_params(2),
        name="flash",
    )(q, k, vt)


def _flash_shifted_kernel(q_ref, k_ref, vt_ref, o_ref, p_ref):
    tq = q_ref.shape[1]
    T = k_ref.shape[2]
    tk = p_ref.shape[1]
    dh = ATT_HEAD_DIM
    ones = jnp.ones((2 * SUBLANES, tk), BF16)
    nq = KV_REP * tq
    nk = T // tk

    def probs(g, c, slot):
        q4 = jnp.concatenate([q_ref[0, :, (g * KV_REP + r) * LANES:(g * KV_REP + r + 1) * LANES]
                              for r in range(KV_REP)], axis=0)
        k0 = pl.multiple_of(c * tk, tk)
        s = lax.dot_general(k_ref[0, g, pl.ds(k0, tk), :], q4, (((1,), (1,)), ((), ())),
                            preferred_element_type=F32)
        p_ref[slot] = jnp.exp2(s).astype(BF16)

    def accumulate(g, c, slot, acc):
        k0 = pl.multiple_of(c * tk, tk)
        va = jnp.concatenate([vt_ref[0, g * dh:(g + 1) * dh, pl.ds(k0, tk)], ones], axis=0)
        return acc + jnp.dot(va, p_ref[slot], preferred_element_type=F32)

    probs(0, 0, 0)
    per_trip = min(ATT_TRIP, nk)
    for g in range(N_KV_HEADS):

        def body(i, acc, g=g):
            c = per_trip * i
            for u in range(per_trip):
                probs(g, c + u + 1, (u + 1) % 2)
                acc = accumulate(g, c + u, u % 2, acc)
            return acc

        acc = lax.fori_loop(0, nk // per_trip - 1, body, jnp.zeros((dh + 2 * SUBLANES, nq), F32))
        for u in range(per_trip):
            c = nk - per_trip + u
            if c + 1 < nk:
                probs(g, c + 1, (u + 1) % 2)
            elif g + 1 < N_KV_HEADS:
                probs(g + 1, 0, 0)
            acc = accumulate(g, c, u % 2, acc)
        o4 = acc[:dh] / acc[dh:dh + 1]
        ot = jnp.concatenate([o4[:, r * tq:(r + 1) * tq] for r in range(KV_REP)], axis=0)
        o_ref[0, :, g * KV_REP * dh:(g + 1) * KV_REP * dh] = ot.T.astype(BF16)


def _flash_shifted(q, k, vt):
    b, T, _ = q.shape
    qd = N_HEADS * ATT_HEAD_DIM
    tq = min(ATT_Q_TILE_SHIFTED, T)
    tk = min(ATT_K_TILE, T // 2)
    per_trip = min(ATT_TRIP, T // tk)
    assert T % tk == 0 and per_trip % 2 == 0 and (T // tk) % per_trip == 0, (T, tk, per_trip)
    return pl.pallas_call(
        _flash_shifted_kernel,
        grid=(b, T // tq),
        in_specs=[pl.BlockSpec((1, tq, N_HEADS * LANES), lambda i, t: (i, t, 0)),
                  pl.BlockSpec((1, N_KV_HEADS, T, LANES), lambda i, t: (i, 0, 0, 0)),
                  pl.BlockSpec((1, N_KV_HEADS * ATT_HEAD_DIM, T), lambda i, t: (i, 0, 0))],
        out_specs=pl.BlockSpec((1, tq, qd), lambda i, t: (i, t, 0)),
        out_shape=jax.ShapeDtypeStruct((b, T, qd), BF16),
        scratch_shapes=[pltpu.VMEM((2, tk, KV_REP * tq), BF16)],
        compiler_params=_params(2),
        name="flash_shifted",
    )(q, k, vt)


def _rope_tables(T):
    t = jnp.arange(T, dtype=jnp.int32)
    pos = jnp.stack([(t // GRID_W).astype(F32), (t % GRID_W).astype(F32)], axis=1)
    inv_freq = ROPE_THETA ** (-jnp.arange(0, AXIS_DIM, 2, dtype=F32) / AXIS_DIM)
    ang = pos[:, :, None] * inv_freq[None, None, :]
    cos = jnp.cos(ang)
    sin = jnp.sin(ang)
    cos_d = jnp.concatenate([cos, cos], axis=-1).reshape(T, ATT_HEAD_DIM)
    sin_d = jnp.concatenate([-sin, sin], axis=-1).reshape(T, ATT_HEAD_DIM)
    return jnp.tile(cos_d, (1, LANES // ATT_HEAD_DIM)), jnp.tile(sin_d, (1, LANES // ATT_HEAD_DIM))


def _group_order(v):
    return v.reshape(2, SSM_GROUPS, HEADS_PER_GROUP).transpose(1, 0, 2).reshape(SSM_GROUPS, 2 * HEADS_PER_GROUP)


def kernel(x_prompt, x_sample, norm_g, ffn_w_gate, ffn_w_up, ffn_w_down, ssm_w_in, ssm_conv_w,
           ssm_conv_b, ssm_dt_bias, ssm_A_log, ssm_D, ssm_norm_g, ssm_w_out, attn_w_qkv,
           attn_q_norm, attn_k_norm, attn_w_out, final_norm):
    nb_prompt, T, _ = x_prompt.shape
    b = nb_prompt + x_sample.shape[0]
    n = b * T
    x = (x_prompt.reshape(nb_prompt * T, D_MODEL), x_sample.reshape(n - nb_prompt * T, D_MODEL))
    depth = norm_g.shape[0]

    wg = ffn_w_gate.astype(BF16)
    wu = ffn_w_up.astype(BF16)
    wd = ffn_w_down.astype(BF16)
    fg = final_norm.reshape(1, D_MODEL)
    cos, sin = _rope_tables(T)
    half = jnp.arange(LANES) // ATT_HEAD_DIM
    head_ones = (half[:, None] == half[None, :]).astype(BF16)

    for i in range(depth):
        j = i // 2
        x = _ffn(x, norm_g[i, 0].reshape(1, D_MODEL), wg[i, 0], wu[i, 0], wd[i, 0], fg, False,
                 nb_prompt * T)
        g_mix = norm_g[i, 1].reshape(1, D_MODEL)
        if i % 2 == 0:
            w_in = ssm_w_in[j]
            wz = w_in[:, :D_INNER].astype(BF16)
            wx = w_in[:, D_INNER:D_INNER + CONV_DIM].astype(BF16)
            wdt = w_in[:, D_INNER + CONV_DIM:].reshape(D_MODEL, 2, SSM_GROUPS, HEADS_PER_GROUP)
            wdt = wdt.transpose(0, 2, 1, 3).reshape(D_MODEL, 2 * SSM_HEADS)
            wdt = jnp.pad(wdt, ((0, 0), (0, LANES - 2 * SSM_HEADS))).astype(BF16)
            z, xbc, dt_raw = _ssd_in(x.reshape(b, T, D_MODEL), g_mix, wz, wx, wdt,
                                     ssm_conv_w[j], ssm_conv_b[j].reshape(1, CONV_DIM))
            dt_rows = dt_raw[:, :, :2 * SSM_HEADS].reshape(b, T, SSM_GROUPS, 2 * HEADS_PER_GROUP)
            dt_rows = dt_rows.transpose(0, 2, 3, 1)
            y = _ssd(xbc, z, dt_rows,
                     _group_order(ssm_dt_bias[j])[:, :, None],
                     _group_order(ssm_A_log[j])[:, :, None],
                     jnp.repeat(ssm_D[j], SSM_HEAD_DIM).reshape(SSM_GROUPS, GROUP_DIM, 1),
                     ssm_norm_g[j].reshape(SSM_GROUPS, 1, GROUP_DIM))
            mix = (y.reshape(n, D_INNER), ssm_w_out[j].astype(BF16))
        else:
            qg = jnp.tile(attn_q_norm[j], LANES // ATT_HEAD_DIM).reshape(1, LANES)
            kg = jnp.tile(attn_k_norm[j], LANES // ATT_HEAD_DIM).reshape(1, LANES)
            w_qkv = attn_w_qkv[j].astype(BF16)

            def attend(x3, shifted, qg=qg, kg=kg, w_qkv=w_qkv, g_mix=g_mix):
                q, k, vt = _attn_in(x3, g_mix, w_qkv, qg, kg, cos, sin, head_ones, shifted)
                return _flash_shifted(q, k, vt) if shifted else _flash(q, k, vt)

            o = lax.cond(2.0 * _score_bound(qg, kg)[0, 0] < SHIFT_SAFE_LOG2,
                         functools.partial(attend, shifted=True),
                         functools.partial(attend, shifted=False),
                         x.reshape(b, T, D_MODEL))
            mix = (o.reshape(n, N_HEADS * ATT_HEAD_DIM), attn_w_out[j].astype(BF16))
        last = i == depth - 1
        x = _ffn(x, norm_g[i, 2].reshape(1, D_MODEL), wg[i, 1], wu[i, 1], wd[i, 1], fg,
                 last, nb_prompt * T, mix, split_out=last)
    y_prompt, y_sample = x
    return (y_prompt.reshape(x_prompt.shape), y_sample.reshape(x_sample.shape))
```
